```python
import jax, jax.numpy as jnp
from jax import lax
import numpy as np

D_MODEL = 1024
BATCH = 16
SEQ = 4096
DEPTH = 2

CTX_LEN = 256
GRID_W = 64
GLA_HEADS = 4
GLA_DK = 64
GLA_DV = 128
GLA_KW = GLA_HEADS * GLA_DK
GLA_VW = GLA_HEADS * GLA_DV
DECAY_RANK = 16
GATE_NORMALIZER = 16.0
CHUNK = 64
CONV_CH = D_MODEL - GLA_VW
CONV_K = 3
MIX_W = GLA_VW + CONV_CH
PROJ_SIZES = (GLA_KW, GLA_KW, GLA_VW, GLA_VW, DECAY_RANK, DECAY_RANK, CONV_CH, CONV_CH, CONV_CH)
PROJ_W = sum(PROJ_SIZES)
D_FF = 2816
N_EXPERTS = 8
TOP_K = 2
D_FF_EXPERT = 3584
N_DENSE = (DEPTH + 1) // 2
N_MOE = DEPTH // 2
EPS = 1e-6

kernel_name = "hybrid_gla_shortconv_moe_dit"


def rmsnorm(x, g):
    xf = x.astype(jnp.float32)
    y = xf * lax.rsqrt(jnp.mean(xf * xf, axis=-1, keepdims=True) + EPS)
    return (y * g.astype(jnp.float32)).astype(x.dtype)


def modulate(x, shift, scale):
    return x * (1 + scale) + shift


def split_proj(p):
    idx = [int(v) for v in np.cumsum(PROJ_SIZES)[:-1]]
    return jnp.split(p, idx, axis=-1)


def to_heads(t, dh):
    b, l, w = t.shape
    return t.reshape(b, l, w // dh, dh).transpose(0, 2, 1, 3)


def gla_chunk_scan(q, k, v, log_a, s0):
    b_, h_, l_, dk = q.shape
    dv = v.shape[-1]
    nc = l_ // CHUNK

    def to_chunks(t):
        return t.reshape(b_, h_, nc, CHUNK, t.shape[-1]).transpose(2, 0, 1, 3, 4)

    tril = jnp.tril(jnp.ones((CHUNK, CHUNK), dtype=bool))

    def step(s, inp):
        qc, kc, vc, gc = [t.astype(jnp.float32) for t in inp]
        bcum = jnp.cumsum(gc, axis=-2)
        o_inter = jnp.einsum('bhcd,bhde->bhce', qc * jnp.exp(bcum), s)
        diff = bcum[:, :, :, None, :] - bcum[:, :, None, :, :]
        decay = jnp.exp(jnp.where(tril[:, :, None], diff, -jnp.inf))
        attn = jnp.einsum('bhid,bhijd->bhij', qc, decay * kc[:, :, None, :, :])
        o_intra = jnp.einsum('bhij,bhje->bhie', attn, vc)
        b_last = bcum[:, :, -1, :]
        s_new = jnp.exp(b_last)[..., None] * s + jnp.einsum(
            'bhjd,bhje->bhde', kc * jnp.exp(b_last[:, :, None, :] - bcum), vc)
        return s_new, o_inter + o_intra

    s_fin, o = lax.scan(step, s0, (to_chunks(q), to_chunks(k), to_chunks(v), to_chunks(log_a)))
    o = o.transpose(1, 2, 0, 3, 4).reshape(b_, h_, l_, dv)
    return o.astype(q.dtype), s_fin


def gla_bidir(q, k, v, la_f, la_b, s0_f, s0_b):
    o_f, s_f = gla_chunk_scan(q, k, v, la_f, s0_f)
    flip = lambda t: jnp.flip(t, axis=2)
    o_b, s_b = gla_chunk_scan(flip(q), flip(k), flip(v), flip(la_b), s0_b)
    return o_f + flip(o_b), s_f, s_b


def gla_inputs(parts, w_dec, b_dec):
    q, k, v, _, a_f, a_b = parts[:6]
    qh = to_heads(q * (GLA_DK ** -0.5), GLA_DK)
    kh = to_heads(k, GLA_DK)
    vh = to_heads(v, GLA_DV)
    la_f = to_heads(jax.nn.log_sigmoid(a_f @ w_dec[0] + b_dec[0]) / GATE_NORMALIZER, GLA_DK)
    la_b = to_heads(jax.nn.log_sigmoid(a_b @ w_dec[1] + b_dec[1]) / GATE_NORMALIZER, GLA_DK)
    return qh, kh, vh, la_f, la_b


def gla_output(o, g, gain):
    b_, h_, l_, dv = o.shape
    on = rmsnorm(o, gain).transpose(0, 2, 1, 3).reshape(b_, l_, h_ * dv)
    return on * jax.nn.silu(g)


def short_conv(gate_b, gate_c, h, w, n_seg, seg_len):
    b_, l_, ch = h.shape
    u = (gate_c * h).reshape(b_ * n_seg, seg_len, ch)
    y = lax.conv_general_dilated(u, w[:, None, :].astype(u.dtype), window_strides=(1,), padding='SAME',
                                 dimension_numbers=('NWC', 'WIO', 'NWC'), feature_group_count=ch)
    return gate_b * y.reshape(b_, l_, ch)


def swiglu(h, w1, w3, w2):
    return (jax.nn.silu(h @ w1) * (h @ w3)) @ w2


def moe_swiglu(h, w_r, w1e, w3e, w2e):
    logits = (h @ w_r).astype(jnp.float32)
    vals, idx = lax.top_k(logits, TOP_K)
    wts = jax.nn.softmax(vals, axis=-1)
    comb = jnp.sum(jax.nn.one_hot(idx, N_EXPERTS, dtype=jnp.float32) * wts[..., None], axis=-2).astype(h.dtype)
    out = jnp.zeros_like(h)
    for e in range(N_EXPERTS):
        out = out + comb[..., e:e + 1] * swiglu(h, w1e[e], w3e[e], w2e[e])
    return out


def setup_inputs(seed: int = 0) -> dict:
    key = jax.random.key(seed)
    ks = iter(jax.random.split(key, 32))
    nrm = lambda shape, scale: jax.random.normal(next(ks), shape, jnp.float32) * scale
    gain = lambda shape: 1.0 + nrm(shape, 0.02)
    d = D_MODEL
    return {
        "x": nrm((BATCH, SEQ, d), 1.0),
        "c": nrm((BATCH, d), 1.0),
        "ctx": nrm((BATCH, CTX_LEN, d), 1.0),
        "c_ctx": nrm((d,), 1.0),
        "w_mod": nrm((DEPTH, d, 6 * d), d ** -0.5),
        "b_mod": nrm((DEPTH, 6 * d), 0.02),
        "g_mix_pre": gain((DEPTH, d)),
        "g_mix_post": gain((DEPTH, d)),
        "w_in": nrm((DEPTH, d, PROJ_W), d ** -0.5),
        "w_decay": nrm((DEPTH, 2, DECAY_RANK, GLA_KW), DECAY_RANK ** -0.5),
        "b_decay": nrm((DEPTH, 2, GLA_KW), 0.5),
        "gla_norm": gain((DEPTH, GLA_DV)),
        "conv_w": nrm((DEPTH, CONV_K, CONV_CH), CONV_K ** -0.5),
        "w_out": nrm((DEPTH, MIX_W, d), MIX_W ** -0.5),
        "g_ffn_pre": gain((DEPTH, d)),
        "g_ffn_post": gain((DEPTH, d)),
        "w1": nrm((N_DENSE, d, D_FF), d ** -0.5),
        "w3": nrm((N_DENSE, d, D_FF), d ** -0.5),
        "w2": nrm((N_DENSE, D_FF, d), D_FF ** -0.5),
        "w_router": nrm((N_MOE, d, N_EXPERTS), d ** -0.5),
        "e_w1": nrm((N_MOE, N_EXPERTS, d, D_FF_EXPERT), d ** -0.5),
        "e_w3": nrm((N_MOE, N_EXPERTS, d, D_FF_EXPERT), d ** -0.5),
        "e_w2": nrm((N_MOE, N_EXPERTS, D_FF_EXPERT, d), D_FF_EXPERT ** -0.5),
    }


def reference(x, c, ctx, c_ctx, w_mod, b_mod, g_mix_pre, g_mix_post, w_in, w_decay, b_decay, gla_norm,
              conv_w, w_out, g_ffn_pre, g_ffn_post, w1, w3, w2, w_router, e_w1, e_w3, e_w2):
    b_, l_, _ = x.shape
    rows = l_ // GRID_W
    ctx_len = ctx.shape[1]
    silu_c = jax.nn.silu(c)
    silu_cc = jax.nn.silu(c_ctx)
    xc = ctx
    for i in range(DEPTH):
        last = i == DEPTH - 1
        sh1, sc1, gt1, sh2, sc2, gt2 = jnp.split((silu_c @ w_mod[i] + b_mod[i])[:, None, :], 6, axis=-1)
        csh1, csc1, cgt1, csh2, csc2, cgt2 = jnp.split(silu_cc @ w_mod[i] + b_mod[i], 6, axis=-1)

        p = split_proj(modulate(rmsnorm(x, g_mix_pre[i]), sh1, sc1) @ w_in[i])
        pc = split_proj(modulate(rmsnorm(xc, g_mix_pre[i]), csh1, csc1) @ w_in[i])

        s_zero = jnp.zeros((ctx.shape[0], GLA_HEADS, GLA_DK, GLA_DV), jnp.float32)
        o_c, s_f, s_b = gla_bidir(*gla_inputs(pc, w_decay[i], b_decay[i]), s_zero, s_zero)
        o_x, _, _ = gla_bidir(*gla_inputs(p, w_decay[i], b_decay[i]), s_f, s_b)

        y_x = jnp.concatenate([
            gla_output(o_x, p[3], gla_norm[i]),
            short_conv(p[6], p[7], p[8], conv_w[i], rows, GRID_W),
        ], axis=-1) @ w_out[i]
        x = x + gt1 * rmsnorm(y_x, g_mix_post[i])
        if not last:
            y_c = jnp.concatenate([
                gla_output(o_c, pc[3], gla_norm[i]),
                short_conv(pc[6], pc[7], pc[8], conv_w[i], 1, ctx_len),
            ], axis=-1) @ w_out[i]
            xc = xc + cgt1 * rmsnorm(y_c, g_mix_post[i])

        j = i // 2
        if i % 2 == 0:
            ffn = lambda h: swiglu(h, w1[j], w3[j], w2[j])
        else:
            ffn = lambda h: moe_swiglu(h, w_router[j], e_w1[j], e_w3[j], e_w2[j])
        x = x + gt2 * rmsnorm(ffn(modulate(rmsnorm(x, g_ffn_pre[i]), sh2, sc2)), g_ffn_post[i])
        if not last:
            xc = xc + cgt2 * rmsnorm(ffn(modulate(rmsnorm(xc, g_ffn_pre[i]), csh2, csc2)), g_ffn_post[i])
    return x
```

```python
import functools

import numpy as np
import jax
import jax.numpy as jnp
from jax import lax
from jax.experimental import pallas as pl
from jax.experimental.pallas import tpu as pltpu

F32 = jnp.float32
BF16 = jnp.bfloat16
HIGHEST = lax.Precision.HIGHEST

D_MODEL = 1024
GLA_HEADS = 4
GLA_DK = 64
GLA_DV = 128
GLA_KW = GLA_HEADS * GLA_DK
GLA_VW = GLA_HEADS * GLA_DV
DECAY_RANK = 16
GATE_NORMALIZER = 16.0
CHUNK = 64
CONV_CH = D_MODEL - GLA_VW
GRID_W = 64
N_EXPERTS = 8
EPS = 1e-6
LANES = 128
N_LEVELS = 6
MOD_ROWS = 24
VMEM_LIMIT = 52 * 1024 * 1024


def _cparams(sem):
    return pltpu.CompilerParams(dimension_semantics=sem, vmem_limit_bytes=VMEM_LIMIT)


def _rms(x, gain):
    return x * lax.rsqrt(jnp.mean(x * x, axis=-1, keepdims=True) + EPS) * gain


def _silu(x):
    return x / (1.0 + jnp.exp(-x))


def _mod_kernel(c_ref, w_ref, b_ref, o_ref):
    s = _silu(c_ref[...])
    o_ref[...] = jnp.dot(s, w_ref[...], precision=HIGHEST, preferred_element_type=F32) + b_ref[...]


def _modulation(c_all, w_mod, b_mod):
    depth, d, n = w_mod.shape
    tn = 1536
    return pl.pallas_call(
        _mod_kernel,
        grid=(depth, n // tn),
        in_specs=[
            pl.BlockSpec((MOD_ROWS, d), lambda i, j: (0, 0)),
            pl.BlockSpec((None, d, tn), lambda i, j: (i, 0, j)),
            pl.BlockSpec((None, 1, tn), lambda i, j: (i, 0, j)),
        ],
        out_specs=pl.BlockSpec((None, MOD_ROWS, tn), lambda i, j: (i, 0, j)),
        out_shape=jax.ShapeDtypeStruct((depth, MOD_ROWS, n), F32),
        compiler_params=_cparams(("parallel", "parallel")),
        name="modulation",
    )(c_all, w_mod, b_mod.reshape(depth, 1, n))


def _proj_kernel(x_ref, sh_ref, sc_ref, g_ref, wm_ref, wa_ref, wc_ref, wd_ref, bd_ref, cw_ref,
                 q_ref, k_ref, v_ref, gate_ref, gf_ref, gb_ref, yc_ref, *, seg):
    h = _rms(x_ref[...], g_ref[...]) * (1.0 + sc_ref[...]) + sh_ref[...]
    hb = h.astype(BF16)
    p = jnp.dot(hb, wm_ref[...], preferred_element_type=F32)
    q_ref[...] = (p[:, :GLA_KW] * (GLA_DK ** -0.5)).astype(BF16)
    k_ref[...] = p[:, GLA_KW:2 * GLA_KW].astype(BF16)
    v_ref[...] = p[:, 2 * GLA_KW:2 * GLA_KW + GLA_VW].astype(BF16)
    gate_ref[...] = p[:, 2 * GLA_KW + GLA_VW:].astype(BF16)
    pa = jnp.dot(hb, wa_ref[...], preferred_element_type=F32)
    xd = jnp.dot(pa, wd_ref[...], precision=HIGHEST, preferred_element_type=F32) + bd_ref[...]
    ls = (jnp.minimum(xd, 0.0) - jnp.log1p(jnp.exp(-jnp.abs(xd)))) * (1.0 / GATE_NORMALIZER)
    gf_ref[...] = ls[:, :GLA_KW]
    gb_ref[...] = ls[:, GLA_KW:]
    pc = jnp.dot(hb, wc_ref[...], preferred_element_type=F32)
    u = pc[:, CONV_CH:2 * CONV_CH] * pc[:, 2 * CONV_CH:]
    tm = u.shape[0]
    row = lax.broadcasted_iota(jnp.int32, u.shape, 0) & (seg - 1)
    u_prev = jnp.where(row == 0, 0.0, pltpu.roll(u, 1, 0))
    u_next = jnp.where(row == seg - 1, 0.0, pltpu.roll(u, tm - 1, 0))
    cw = cw_ref[...]
    yc = pc[:, :CONV_CH] * (cw[0:1] * u_prev + cw[1:2] * u + cw[2:3] * u_next)
    yc_ref[...] = yc.astype(BF16)


def _project(x2, mod3, mod_row, gain, wm, wa, wc, wd, bd, cw, *, tm, seg):
    n = x2.shape[0]
    d = D_MODEL
    row = lambda c: pl.BlockSpec((None, 1, d), lambda i: (mod_row(i), 0, c))
    const = lambda a: pl.BlockSpec(a.shape, lambda i: (0,) * a.ndim)
    tile = lambda w: pl.BlockSpec((tm, w), lambda i: (i, 0))
    shapes = [(GLA_KW, BF16), (GLA_KW, BF16), (GLA_VW, BF16), (GLA_VW, BF16),
              (GLA_KW, F32), (GLA_KW, F32), (CONV_CH, BF16)]
    return pl.pallas_call(
        functools.partial(_proj_kernel, seg=seg),
        grid=(n // tm,),
        in_specs=[tile(d), row(0), row(1), const(gain), const(wm), const(wa), const(wc),
                  const(wd), const(bd), const(cw)],
        out_specs=[tile(w) for w, _ in shapes],
        out_shape=[jax.ShapeDtypeStruct((n, w), t) for w, t in shapes],
        compiler_params=_cparams(("parallel",)),
        name="in_proj",
    )(x2, mod3, mod3, gain, wm, wa, wc, wd, bd, cw)


def _gla_constants():
    c = CHUNK
    main = np.zeros((2 * N_LEVELS + 2, c, 2 * c), np.float32)
    lmask = np.zeros((N_LEVELS + 1, c, c), np.float32)
    for lvl in range(N_LEVELS):
        s = c >> (lvl + 1)
        for i in range(c):
            mid = (i // (2 * s)) * 2 * s + s
            if i >= mid:
                main[2 * lvl, i, mid:i + 1] = 1.0
                main[2 * lvl + 1, i, c + mid:c + i] = 1.0
            else:
                main[2 * lvl, i, c + i:c + mid] = 1.0
                main[2 * lvl + 1, i, i + 1:mid] = 1.0
            for j in range(c):
                same = (j // (2 * s)) == (i // (2 * s))
                lmask[lvl, i, j] = float(same and ((i >= mid) != (j >= mid)))
    lmask[N_LEVELS] = 2.0 * np.eye(c)
    bwd = np.zeros((2, c, c), np.float32)
    for i in range(c):
        main[2 * N_LEVELS, i, :i + 1] = 1.0
        main[2 * N_LEVELS + 1, i, i + 1:c] = 1.0
        bwd[0, i, i:] = 1.0
        bwd[1, i, :i] = 1.0
    main = main.reshape(-1, 2 * c)
    bwd = bwd.reshape(-1, c)
    lmask = np.concatenate([lmask, lmask], axis=-1)
    return jnp.asarray(main, BF16), jnp.asarray(bwd, BF16), jnp.asarray(lmask, F32)


def _split_dot(n, g):
    hi = g.astype(BF16)
    lo = (g - hi.astype(F32)).astype(BF16)
    return (jnp.dot(n, hi, preferred_element_type=F32) + jnp.dot(n, lo, preferred_element_type=F32))


_NT = (((1,), (1,)), ((), ()))
_TN = (((0,), (0,)), ((), ()))


def _gla_kernel(qf_ref, kf_ref, vf_ref, gff_ref, gbf_ref, qb_ref, kb_ref, vb_ref, gbb_ref,
                gate_ref, gain_ref, s0f_ref, s0b_ref, nmain_ref, nbwd_ref, lmask_ref,
                out_ref, sf_out_ref, sb_out_ref, acc_ref, sf_ref, sb_ref, *, cps):
    n = pl.program_id(1)
    nb = pl.num_programs(1)
    c = CHUNK
    pw = 2 * GLA_DK
    vw = 2 * GLA_DV

    @pl.when(n == 0)
    def _():
        acc_ref[...] = jnp.zeros_like(acc_ref)
        sf_ref[...] = s0f_ref[...]
        sb_ref[...] = s0b_ref[...]

    klane = lax.broadcasted_iota(jnp.int32, (c, GLA_KW), 1)
    k_even = (klane & (pw - 1)) < GLA_DK
    vlane = lax.broadcasted_iota(jnp.int32, (c, vw), 1)
    v_low = vlane < GLA_DV
    srow = lax.broadcasted_iota(jnp.int32, (vw, pw), 0)
    scol = lax.broadcasted_iota(jnp.int32, (vw, pw), 1)
    s_diag = (srow < GLA_DV) == (scol < GLA_DK)
    zero_b = jnp.zeros((), BF16)

    def inter(qi, ki, dec, v, s_ref):
        outs = []
        for p in range(2):
            st = s_ref[p]
            outs.append(lax.dot_general(qi[:, p * pw:(p + 1) * pw], st.astype(BF16), _NT,
                                        preferred_element_type=F32))
            upd = lax.dot_general(v[:, p * vw:(p + 1) * vw], ki[:, p * pw:(p + 1) * pw], _TN,
                                  preferred_element_type=F32)
            s_ref[p] = st * dec[:, p * pw:(p + 1) * pw] + jnp.where(s_diag, upd, 0.0)
        return jnp.concatenate(outs, axis=1)

    def fwd_chunk(j, carry):
        r0 = pl.multiple_of(j * c, c)
        rows = pl.ds(r0, c)
        qb16 = qf_ref[rows, :]
        kb16 = kf_ref[rows, :]
        v = vf_ref[rows, :]
        q = qb16.astype(F32)
        k = kb16.astype(F32)
        g = jnp.concatenate([gff_ref[rows, :], gbf_ref[rows, :]], axis=0)
        e = _split_dot(nmain_ref[...], g)
        att = [jnp.zeros((c, pw), F32), jnp.zeros((c, pw), F32)]
        for lvl in range(N_LEVELS + 1):
            if lvl < N_LEVELS:
                qt = (q * jnp.exp(e[2 * lvl * c:(2 * lvl + 1) * c])).astype(BF16)
                kt = (k * jnp.exp(e[(2 * lvl + 1) * c:(2 * lvl + 2) * c])).astype(BF16)
            else:
                qt, kt = qb16, kb16
            ke = jnp.where(k_even, kt, zero_b)
            ko = jnp.where(k_even, zero_b, kt)
            m = lmask_ref[lvl]
            for p in range(2):
                kbd = jnp.concatenate([ke[:, p * pw:(p + 1) * pw], ko[:, p * pw:(p + 1) * pw]], axis=0)
                att[p] = att[p] + m * lax.dot_general(qt[:, p * pw:(p + 1) * pw], kbd, _NT,
                                                      preferred_element_type=F32)
        outs = []
        for p in range(2):
            vp = v[:, p * vw:(p + 1) * vw]
            vbd = jnp.concatenate([jnp.where(v_low, vp, zero_b), jnp.where(v_low, zero_b, vp)], axis=0)
            outs.append(jnp.dot(att[p].astype(BF16), vbd, preferred_element_type=F32))
        o = jnp.concatenate(outs, axis=1)
        base = 2 * N_LEVELS * c
        qi = (q * jnp.exp(e[base:base + c])).astype(BF16)
        ki = (k * jnp.exp(e[base + c:base + 2 * c])).astype(BF16)
        dec = jnp.exp(e[base + c - 1:base + c])
        o = o + inter(qi, ki, dec, v, sf_ref)
        a0 = pl.multiple_of(n * (cps * c) + r0, c)
        acc_ref[pl.ds(a0, c), :] += o
        return carry

    def bwd_chunk(j, carry):
        r0 = pl.multiple_of((cps - 1 - j) * c, c)
        rows = pl.ds(r0, c)
        q = qb_ref[rows, :].astype(F32)
        k = kb_ref[rows, :].astype(F32)
        v = vb_ref[rows, :]
        e = _split_dot(nbwd_ref[...], gbb_ref[rows, :])
        qi = (q * jnp.exp(e[:c])).astype(BF16)
        ki = (k * jnp.exp(e[c:])).astype(BF16)
        dec = jnp.exp(e[0:1])
        o = inter(qi, ki, dec, v, sb_ref)
        a0 = pl.multiple_of((nb - 1 - n) * (cps * c) + r0, c)
        acc_ref[pl.ds(a0, c), :] += o
        return carry

    lax.fori_loop(0, cps, fwd_chunk, 0)
    lax.fori_loop(0, cps, bwd_chunk, 0)

    @pl.when(n == nb - 1)
    def _():
        sf_out_ref[...] = sf_ref[...]
        sb_out_ref[...] = sb_ref[...]
        gain = gain_ref[...]
        rb = 256

        def gate_rows(i, carry):
            rows = pl.ds(pl.multiple_of(i * rb, rb), rb)
            o = acc_ref[rows, :]
            gt = gate_ref[rows, :].astype(F32)
            normed = jnp.concatenate(
                [_rms(o[:, h * GLA_DV:(h + 1) * GLA_DV], gain) for h in range(GLA_HEADS)], axis=1)
            out_ref[rows, :] = (normed * _silu(gt)).astype(BF16)
            return carry

        lax.fori_loop(0, acc_ref.shape[0] // rb, gate_rows, 0)


def _gla(q, k, v, gf, gb, gate, gain, s0f, s0b, consts, *, cps):
    b, l, _ = q.shape
    t = cps * CHUNK
    nb = l // t
    nmain, nbwd, lmask = consts
    fwd = lambda w: pl.BlockSpec((None, t, w), lambda i, n: (i, n, 0))
    bwd = lambda w: pl.BlockSpec((None, t, w), lambda i, n: (i, nb - 1 - n, 0))
    whole = lambda w: pl.BlockSpec((None, l, w), lambda i, n: (i, 0, 0))
    state = pl.BlockSpec((None, 2, 2 * GLA_DV, 2 * GLA_DK), lambda i, n: (i, 0, 0, 0))
    const = lambda a: pl.BlockSpec(a.shape, lambda i, n: (0,) * a.ndim)
    sshape = jax.ShapeDtypeStruct((b, 2, 2 * GLA_DV, 2 * GLA_DK), F32)
    return pl.pallas_call(
        functools.partial(_gla_kernel, cps=cps),
        grid=(b, nb),
        in_specs=[fwd(GLA_KW), fwd(GLA_KW), fwd(GLA_VW), fwd(GLA_KW), fwd(GLA_KW),
                  bwd(GLA_KW), bwd(GLA_KW), bwd(GLA_VW), bwd(GLA_KW),
                  whole(GLA_VW), const(gain), state, state, const(nmain), const(nbwd), const(lmask)],
        out_specs=[whole(GLA_VW), state, state],
        out_shape=[jax.ShapeDtypeStruct((b, l, GLA_VW), BF16), sshape, sshape],
        scratch_shapes=[pltpu.VMEM((l, GLA_VW), F32),
                        pltpu.VMEM((2, 2 * GLA_DV, 2 * GLA_DK), F32),
                        pltpu.VMEM((2, 2 * GLA_DV, 2 * GLA_DK), F32)],
        compiler_params=_cparams(("parallel", "arbitrary")),
        name="gla_scan",
    )(q, k, v, gf, gb, q, k, v, gb, gate, gain, s0f, s0b, nmain, nbwd, lmask)


def _outproj_kernel(gla_ref, yc_ref, x_ref, gt_ref, gp_ref, w_ref, o_ref):
    y = (jnp.dot(gla_ref[...], w_ref[:GLA_VW, :], preferred_element_type=F32)
         + jnp.dot(yc_ref[...], w_ref[GLA_VW:, :], preferred_element_type=F32))
    o_ref[...] = x_ref[...] + gt_ref[...] * _rms(y, gp_ref[...])


def _out_project(gla, yc, x2, mod3, mod_row, gpost, w_out, *, tm):
    n, d = x2.shape
    tile = lambda w: pl.BlockSpec((tm, w), lambda i: (i, 0))
    const = lambda a: pl.BlockSpec(a.shape, lambda i: (0,) * a.ndim)
    return pl.pallas_call(
        _outproj_kernel,
        grid=(n // tm,),
        in_specs=[tile(GLA_VW), tile(CONV_CH), tile(d),
                  pl.BlockSpec((None, 1, d), lambda i: (mod_row(i), 0, 2)), const(gpost), const(w_out)],
        out_specs=tile(d),
        out_shape=jax.ShapeDtypeStruct((n, d), F32),
        compiler_params=_cparams(("parallel",)),
        name="out_proj",
    )(gla, yc, x2, mod3, gpost, w_out)


def _ffn_kernel(x_ref, sh_ref, sc_ref, gt_ref, gpre_ref, gpost_ref, wr_ref, w1_ref, w3_ref, w2_ref,
                o_ref, hb_ref, acc_ref, comb_ref, *, moe):
    e = pl.program_id(1)
    f = pl.program_id(2)

    @pl.when((e == 0) & (f == 0))
    def _():
        h = _rms(x_ref[...], gpre_ref[...]) * (1.0 + sc_ref[...]) + sh_ref[...]
        hb_ref[...] = h.astype(BF16)
        acc_ref[...] = jnp.zeros_like(acc_ref)
        if moe:
            logits = jnp.dot(h, wr_ref[...], precision=HIGHEST, preferred_element_type=F32)
            lane = lax.broadcasted_iota(jnp.int32, logits.shape, 1).astype(F32)
            logits = jnp.where(lane < N_EXPERTS, logits, -jnp.inf)
            m1 = jnp.max(logits, axis=-1, keepdims=True)
            i1 = jnp.min(jnp.where(logits == m1, lane, float(LANES)), axis=-1, keepdims=True)
            rest = jnp.where(lane == i1, -jnp.inf, logits)
            m2 = jnp.max(rest, axis=-1, keepdims=True)
            i2 = jnp.min(jnp.where(rest == m2, lane, float(LANES)), axis=-1, keepdims=True)
            e2 = jnp.exp(m2 - m1)
            den = 1.0 + e2
            comb_ref[...] = jnp.where(lane == i1, 1.0 / den, 0.0) + jnp.where(lane == i2, e2 / den, 0.0)

    hb = hb_ref[...]
    a = jnp.dot(hb, w1_ref[...], preferred_element_type=F32)
    b = jnp.dot(hb, w3_ref[...], preferred_element_type=F32)
    y = jnp.dot((_silu(a) * b).astype(BF16), w2_ref[...], preferred_element_type=F32)
    if moe:
        comb = comb_ref[...]
        lane = lax.broadcasted_iota(jnp.int32, comb.shape, 1)
        y = y * jnp.sum(jnp.where(lane == e, comb, 0.0), axis=-1, keepdims=True)
    acc_ref[...] += y

    @pl.when((e == pl.num_programs(1) - 1) & (f == pl.num_programs(2) - 1))
    def _():
        o_ref[...] = x_ref[...] + gt_ref[...] * _rms(acc_ref[...], gpost_ref[...])


def _channel_mix(x2, mod3, mod_row, gpre, gpost, w_r, w1, w3, w2, *, tm, tf, moe):
    n, d = x2.shape
    ne, _, ff = w1.shape
    nf = ff // tf
    row = lambda c: pl.BlockSpec((None, 1, d), lambda i, e, f: (mod_row(i), 0, c))
    const = lambda a: pl.BlockSpec(a.shape, lambda i, e, f: (0,) * a.ndim)
    tile = pl.BlockSpec((tm, d), lambda i, e, f: (i, 0))
    return pl.pallas_call(
        functools.partial(_ffn_kernel, moe=moe),
        grid=(n // tm, ne, nf),
        in_specs=[tile, row(3), row(4), row(5), const(gpre), const(gpost), const(w_r),
                  pl.BlockSpec((None, d, tf), lambda i, e, f: (e, 0, f)),
                  pl.BlockSpec((None, d, tf), lambda i, e, f: (e, 0, f)),
                  pl.BlockSpec((None, tf, d), lambda i, e, f: (e, f, 0))],
        out_specs=tile,
        out_shape=jax.ShapeDtypeStruct((n, d), F32),
        scratch_shapes=[pltpu.VMEM((tm, d), BF16), pltpu.VMEM((tm, d), F32), pltpu.VMEM((tm, LANES), F32)],
        compiler_params=_cparams(("parallel", "arbitrary", "arbitrary")),
        name="moe_mix" if moe else "dense_mix",
    )(x2, mod3, mod3, mod3, gpre, gpost, w_r, w1, w3, w2)


def kernel(x, c, ctx, c_ctx, w_mod, b_mod, g_mix_pre, g_mix_post, w_in, w_decay, b_decay, gla_norm,
           conv_w, w_out, g_ffn_pre, g_ffn_post, w1, w3, w2, w_router, e_w1, e_w3, e_w2):
    bsz, seq, d = x.shape
    ctx_len = ctx.shape[1]
    depth = w_mod.shape[0]
    assert d == D_MODEL and seq % (4 * CHUNK) == 0 and ctx_len % CHUNK == 0 and bsz + 1 <= MOD_ROWS

    c_all = jnp.zeros((MOD_ROWS, d), F32).at[:bsz].set(c).at[bsz].set(c_ctx)
    mod = _modulation(c_all, w_mod, b_mod)
    consts = _gla_constants()

    tm_x = 512
    tm_c = ctx_len
    x_row = lambda i: i // (seq // tm_x)
    c_row = lambda i: bsz
    x2 = x.reshape(bsz * seq, d)
    xc2 = ctx.reshape(bsz * ctx_len, d)
    zero_state = jnp.zeros((bsz, 2, 2 * GLA_DV, 2 * GLA_DK), F32)
    row2 = lambda a: a.reshape(1, -1)

    o_q, o_k, o_v = 0, GLA_KW, 2 * GLA_KW
    o_g = o_v + GLA_VW
    o_a = o_g + GLA_VW
    o_c = o_a + 2 * DECAY_RANK

    for i in range(depth):
        last = i == depth - 1
        mod3 = mod[i].reshape(MOD_ROWS, 1, 6 * d)
        wi = w_in[i]
        wm = wi[:, :o_a].astype(BF16)
        wa = jnp.zeros((d, LANES), F32).at[:, :2 * DECAY_RANK].set(wi[:, o_a:o_c]).astype(BF16)
        wc = wi[:, o_c:].astype(BF16)
        wd = jnp.zeros((LANES, 2 * GLA_KW), F32)
        wd = wd.at[:DECAY_RANK, :GLA_KW].set(w_decay[i, 0]).at[DECAY_RANK:2 * DECAY_RANK, GLA_KW:].set(w_decay[i, 1])
        bd = b_decay[i].reshape(1, 2 * GLA_KW)
        wo = w_out[i].astype(BF16)
        gain = row2(gla_norm[i])

        def mix(tokens, mod_row, tm, seg, nseq, slen, s0f, s0b, cps):
            q, k, v, gate, gf, gb, yc = _project(tokens, mod3, mod_row, row2(g_mix_pre[i]), wm, wa, wc, wd, bd,
                                                 conv_w[i], tm=tm, seg=seg)
            r3 = lambda a: a.reshape(nseq, slen, a.shape[-1])
            o, sf, sb = _gla(r3(q), r3(k), r3(v), r3(gf), r3(gb), r3(gate), gain, s0f, s0b, consts, cps=cps)
            return o.reshape(nseq * slen, GLA_VW), yc, sf, sb

        o_c_, yc_c, s_f, s_b = mix(xc2, c_row, tm_c, ctx_len, bsz, ctx_len, zero_state, zero_state,
                                   ctx_len // CHUNK)
        o_x, yc_x, _, _ = mix(x2, x_row, tm_x, GRID_W, bsz, seq, s_f, s_b, 4)
        x2 = _out_project(o_x, yc_x, x2, mod3, x_row, row2(g_mix_post[i]), wo, tm=tm_x)
        if not last:
            xc2 = _out_project(o_c_, yc_c, xc2, mod3, c_row, row2(g_mix_post[i]), wo, tm=tm_c)

        j = i // 2
        if i % 2 == 0:
            w_r = jnp.zeros((d, LANES), F32)
            f1, f3, f2 = w1[j][None].astype(BF16), w3[j][None].astype(BF16), w2[j][None].astype(BF16)
            tf, moe = 1408, False
        else:
            w_r = jnp.zeros((d, LANES), F32).at[:, :N_EXPERTS].set(w_router[j])
            f1, f3, f2 = e_w1[j].astype(BF16), e_w3[j].astype(BF16), e_w2[j].astype(BF16)
            tf, moe = 896, True
        ffn = functools.partial(_channel_mix, gpre=row2(g_ffn_pre[i]), gpost=row2(g_ffn_post[i]), w_r=w_r,
                                w1=f1, w3=f3, w2=f2, tf=tf, moe=moe)
        x2 = ffn(x2, mod3, x_row, tm=tm_x)
        if not last:
            xc2 = ffn(xc2, mod3, c_row, tm=tm_c)
    return x2.reshape(bsz, seq, d)
```

```python
import functools

import numpy as np
import jax
import jax.numpy as jnp
from jax import lax
from jax.experimental import pallas as pl
from jax.experimental.pallas import tpu as pltpu

F32 = jnp.float32
BF16 = jnp.bfloat16
HIGHEST = lax.Precision.HIGHEST

D_MODEL = 1024
GLA_HEADS = 4
GLA_DK = 64
GLA_DV = 128
GLA_KW = GLA_HEADS * GLA_DK
GLA_VW = GLA_HEADS * GLA_DV
DECAY_RANK = 16
GATE_NORMALIZER = 16.0
CHUNK = 64
CONV_CH = D_MODEL - GLA_VW
GRID_W = 64
N_EXPERTS = 8
EPS = 1e-6
LANES = 128
N_LEVELS = 6
MOD_ROWS = 24
VMEM_LIMIT = 52 * 1024 * 1024


def _cparams(sem):
    return pltpu.CompilerParams(dimension_semantics=sem, vmem_limit_bytes=VMEM_LIMIT)


def _rms(x, gain):
    return x * lax.rsqrt(jnp.mean(x * x, axis=-1, keepdims=True) + EPS) * gain


def _silu(x):
    return x / (1.0 + jnp.exp(-x))


def _mod_kernel(c_ref, w_ref, b_ref, o_ref):
    s = _silu(c_ref[...])
    o_ref[...] = jnp.dot(s, w_ref[...], precision=HIGHEST, preferred_element_type=F32) + b_ref[...]


def _modulation(c_all, w_mod, b_mod):
    depth, d, n = w_mod.shape
    tn = 1536
    return pl.pallas_call(
        _mod_kernel,
        grid=(depth, n // tn),
        in_specs=[
            pl.BlockSpec((MOD_ROWS, d), lambda i, j: (0, 0)),
            pl.BlockSpec((None, d, tn), lambda i, j: (i, 0, j)),
            pl.BlockSpec((None, 1, tn), lambda i, j: (i, 0, j)),
        ],
        out_specs=pl.BlockSpec((None, MOD_ROWS, tn), lambda i, j: (i, 0, j)),
        out_shape=jax.ShapeDtypeStruct((depth, MOD_ROWS, n), F32),
        compiler_params=_cparams(("parallel", "parallel")),
        name="modulation",
    )(c_all, w_mod, b_mod.reshape(depth, 1, n))


def _proj_kernel(x_ref, sh_ref, sc_ref, g_ref, wm_ref, wa_ref, wc_ref, wd_ref, bd_ref, cw_ref,
                 q_ref, k_ref, v_ref, gate_ref, gf_ref, gb_ref, yc_ref, *, seg):
    h = _rms(x_ref[...], g_ref[...]) * (1.0 + sc_ref[...]) + sh_ref[...]
    hb = h.astype(BF16)
    p = jnp.dot(hb, wm_ref[...], preferred_element_type=F32)
    q_ref[...] = (p[:, :GLA_KW] * (GLA_DK ** -0.5)).astype(BF16)
    k_ref[...] = p[:, GLA_KW:2 * GLA_KW].astype(BF16)
    v_ref[...] = p[:, 2 * GLA_KW:2 * GLA_KW + GLA_VW].astype(BF16)
    gate_ref[...] = p[:, 2 * GLA_KW + GLA_VW:].astype(BF16)
    pa = jnp.dot(hb, wa_ref[...], preferred_element_type=F32)
    xd = jnp.dot(pa, wd_ref[...], precision=HIGHEST, preferred_element_type=F32) + bd_ref[...]
    ls = (jnp.minimum(xd, 0.0) - jnp.log1p(jnp.exp(-jnp.abs(xd)))) * (1.0 / GATE_NORMALIZER)
    gf_ref[...] = ls[:, :GLA_KW]
    gb_ref[...] = ls[:, GLA_KW:]
    pc = jnp.dot(hb, wc_ref[...], preferred_element_type=F32)
    u = pc[:, CONV_CH:2 * CONV_CH] * pc[:, 2 * CONV_CH:]
    tm = u.shape[0]
    row = lax.broadcasted_iota(jnp.int32, u.shape, 0) & (seg - 1)
    u_prev = jnp.where(row == 0, 0.0, pltpu.roll(u, 1, 0))
    u_next = jnp.where(row == seg - 1, 0.0, pltpu.roll(u, tm - 1, 0))
    cw = cw_ref[...]
    yc = pc[:, :CONV_CH] * (cw[0:1] * u_prev + cw[1:2] * u + cw[2:3] * u_next)
    yc_ref[...] = yc.astype(BF16)


def _project(x2, mod3, mod_row, gain, wm, wa, wc, wd, bd, cw, *, tm, seg):
    n = x2.shape[0]
    d = D_MODEL
    row = lambda c: pl.BlockSpec((None, 1, d), lambda i: (mod_row(i), 0, c))
    const = lambda a: pl.BlockSpec(a.shape, lambda i: (0,) * a.ndim)
    tile = lambda w: pl.BlockSpec((tm, w), lambda i: (i, 0))
    shapes = [(GLA_KW, BF16), (GLA_KW, BF16), (GLA_VW, BF16), (GLA_VW, BF16),
              (GLA_KW, F32), (GLA_KW, F32), (CONV_CH, BF16)]
    return pl.pallas_call(
        functools.partial(_proj_kernel, seg=seg),
        grid=(n // tm,),
        in_specs=[tile(d), row(0), row(1), const(gain), const(wm), const(wa), const(wc),
                  const(wd), const(bd), const(cw)],
        out_specs=[tile(w) for w, _ in shapes],
        out_shape=[jax.ShapeDtypeStruct((n, w), t) for w, t in shapes],
        compiler_params=_cparams(("parallel",)),
        name="in_proj",
    )(x2, mod3, mod3, gain, wm, wa, wc, wd, bd, cw)


def _gla_constants():
    c = CHUNK
    main = np.zeros((2 * N_LEVELS + 2, c, 2 * c), np.float32)
    lmask = np.zeros((N_LEVELS + 1, c, c), np.float32)
    for lvl in range(N_LEVELS):
        s = c >> (lvl + 1)
        for i in range(c):
            mid = (i // (2 * s)) * 2 * s + s
            if i >= mid:
                main[2 * lvl, i, mid:i + 1] = 1.0
                main[2 * lvl + 1, i, c + mid:c + i] = 1.0
            else:
                main[2 * lvl, i, c + i:c + mid] = 1.0
                main[2 * lvl + 1, i, i + 1:mid] = 1.0
            for j in range(c):
                same = (j // (2 * s)) == (i // (2 * s))
                lmask[lvl, i, j] = float(same and ((i >= mid) != (j >= mid)))
    lmask[N_LEVELS] = 2.0 * np.eye(c)
    bwd = np.zeros((2, c, c), np.float32)
    for i in range(c):
        main[2 * N_LEVELS, i, :i + 1] = 1.0
        main[2 * N_LEVELS + 1, i, i + 1:c] = 1.0
        bwd[0, i, i:] = 1.0
        bwd[1, i, :i] = 1.0
    main = main.reshape(-1, 2 * c)
    bwd = bwd.reshape(-1, c)
    lmask = np.concatenate([lmask, lmask], axis=-1)
    return jnp.asarray(main, BF16), jnp.asarray(bwd, BF16), jnp.asarray(lmask, F32)


def _split_dot(n, g):
    hi = g.astype(BF16)
    lo = (g - hi.astype(F32)).astype(BF16)
    return (jnp.dot(n, hi, preferred_element_type=F32) + jnp.dot(n, lo, preferred_element_type=F32))


_NT = (((1,), (1,)), ((), ()))
_TN = (((0,), (0,)), ((), ()))


def _gla_kernel(qf_ref, kf_ref, vf_ref, gff_ref, gbf_ref, qb_ref, kb_ref, vb_ref, gbb_ref,
                gate_ref, gain_ref, s0f_ref, s0b_ref, nmain_ref, nbwd_ref, lmask_ref,
                out_ref, sf_out_ref, sb_out_ref, acc_ref, sf_ref, sb_ref, *, cps):
    n = pl.program_id(1)
    nb = pl.num_programs(1)
    c = CHUNK
    pw = 2 * GLA_DK
    vw = 2 * GLA_DV

    @pl.when(n == 0)
    def _():
        acc_ref[...] = jnp.zeros_like(acc_ref)
        sf_ref[...] = s0f_ref[...]
        sb_ref[...] = s0b_ref[...]

    klane = lax.broadcasted_iota(jnp.int32, (c, GLA_KW), 1)
    k_even = (klane & (pw - 1)) < GLA_DK
    vlane = lax.broadcasted_iota(jnp.int32, (c, vw), 1)
    v_low = vlane < GLA_DV
    srow = lax.broadcasted_iota(jnp.int32, (vw, pw), 0)
    scol = lax.broadcasted_iota(jnp.int32, (vw, pw), 1)
    s_diag = (srow < GLA_DV) == (scol < GLA_DK)
    zero_b = jnp.zeros((), BF16)

    def inter(qi, ki, dec, v, s_ref):
        outs = []
        for p in range(2):
            st = s_ref[p]
            outs.append(lax.dot_general(qi[:, p * pw:(p + 1) * pw], st.astype(BF16), _NT,
                                        preferred_element_type=F32))
            upd = lax.dot_general(v[:, p * vw:(p + 1) * vw], ki[:, p * pw:(p + 1) * pw], _TN,
                                  preferred_element_type=F32)
            s_ref[p] = st * dec[:, p * pw:(p + 1) * pw] + jnp.where(s_diag, upd, 0.0)
        return jnp.concatenate(outs, axis=1)

    def fwd_chunk(j, carry):
        r0 = pl.multiple_of(j * c, c)
        rows = pl.ds(r0, c)
        qb16 = qf_ref[rows, :]
        kb16 = kf_ref[rows, :]
        v = vf_ref[rows, :]
        q = qb16.astype(F32)
        k = kb16.astype(F32)
        g = jnp.concatenate([gff_ref[rows, :], gbf_ref[rows, :]], axis=0)
        e = _split_dot(nmain_ref[...], g)
        att = [jnp.zeros((c, pw), F32), jnp.zeros((c, pw), F32)]
        for lvl in range(N_LEVELS + 1):
            if lvl < N_LEVELS:
                qt = (q * jnp.exp(e[2 * lvl * c:(2 * lvl + 1) * c])).astype(BF16)
                kt = (k * jnp.exp(e[(2 * lvl + 1) * c:(2 * lvl + 2) * c])).astype(BF16)
            else:
                qt, kt = qb16, kb16
            ke = jnp.where(k_even, kt, zero_b)
            ko = jnp.where(k_even, zero_b, kt)
            m = lmask_ref[lvl]
            for p in range(2):
                kbd = jnp.concatenate([ke[:, p * pw:(p + 1) * pw], ko[:, p * pw:(p + 1) * pw]], axis=0)
                att[p] = att[p] + m * lax.dot_general(qt[:, p * pw:(p + 1) * pw], kbd, _NT,
                                                      preferred_element_type=F32)
        outs = []
        for p in range(2):
            vp = v[:, p * vw:(p + 1) * vw]
            vbd = jnp.concatenate([jnp.where(v_low, vp, zero_b), jnp.where(v_low, zero_b, vp)], axis=0)
            outs.append(jnp.dot(att[p].astype(BF16), vbd, preferred_element_type=F32))
        o = jnp.concatenate(outs, axis=1)
        base = 2 * N_LEVELS * c
        qi = (q * jnp.exp(e[base:base + c])).astype(BF16)
        ki = (k * jnp.exp(e[base + c:base + 2 * c])).astype(BF16)
        dec = jnp.exp(e[base + c - 1:base + c])
        o = o + inter(qi, ki, dec, v, sf_ref)
        a0 = pl.multiple_of(n * (cps * c) + r0, c)
        acc_ref[pl.ds(a0, c), :] += o
        return carry

    def bwd_chunk(j, carry):
        r0 = pl.multiple_of((cps - 1 - j) * c, c)
        rows = pl.ds(r0, c)
        q = qb_ref[rows, :].astype(F32)
        k = kb_ref[rows, :].astype(F32)
        v = vb_ref[rows, :]
        e = _split_dot(nbwd_ref[...], gbb_ref[rows, :])
        qi = (q * jnp.exp(e[:c])).astype(BF16)
        ki = (k * jnp.exp(e[c:])).astype(BF16)
        dec = jnp.exp(e[0:1])
        o = inter(qi, ki, dec, v, sb_ref)
        a0 = pl.multiple_of((nb - 1 - n) * (cps * c) + r0, c)
        acc_ref[pl.ds(a0, c), :] += o
        return carry

    lax.fori_loop(0, cps, fwd_chunk, 0)
    lax.fori_loop(0, cps, bwd_chunk, 0)

    @pl.when(n == nb - 1)
    def _():
        sf_out_ref[...] = sf_ref[...]
        sb_out_ref[...] = sb_ref[...]
        gain = gain_ref[...]
        rb = 256

        def gate_rows(i, carry):
            rows = pl.ds(pl.multiple_of(i * rb, rb), rb)
            o = acc_ref[rows, :]
            gt = gate_ref[rows, :].astype(F32)
            normed = jnp.concatenate(
                [_rms(o[:, h * GLA_DV:(h + 1) * GLA_DV], gain) for h in range(GLA_HEADS)], axis=1)
            out_ref[rows, :] = (normed * _silu(gt)).astype(BF16)
            return carry

        lax.fori_loop(0, acc_ref.shape[0] // rb, gate_rows, 0)


def _gla(q, k, v, gf, gb, gate, gain, s0f, s0b, consts, *, cps):
    b, l, _ = q.shape
    t = cps * CHUNK
    nb = l // t
    nmain, nbwd, lmask = consts
    fwd = lambda w: pl.BlockSpec((None, t, w), lambda i, n: (i, n, 0))
    bwd = lambda w: pl.BlockSpec((None, t, w), lambda i, n: (i, nb - 1 - n, 0))
    whole = lambda w: pl.BlockSpec((None, l, w), lambda i, n: (i, 0, 0))
    state = pl.BlockSpec((None, 2, 2 * GLA_DV, 2 * GLA_DK), lambda i, n: (i, 0, 0, 0))
    const = lambda a: pl.BlockSpec(a.shape, lambda i, n: (0,) * a.ndim)
    sshape = jax.ShapeDtypeStruct((b, 2, 2 * GLA_DV, 2 * GLA_DK), F32)
    return pl.pallas_call(
        functools.partial(_gla_kernel, cps=cps),
        grid=(b, nb),
        in_specs=[fwd(GLA_KW), fwd(GLA_KW), fwd(GLA_VW), fwd(GLA_KW), fwd(GLA_KW),
                  bwd(GLA_KW), bwd(GLA_KW), bwd(GLA_VW), bwd(GLA_KW),
                  whole(GLA_VW), const(gain), state, state, const(nmain), const(nbwd), const(lmask)],
        out_specs=[whole(GLA_VW), state, state],
        out_shape=[jax.ShapeDtypeStruct((b, l, GLA_VW), BF16), sshape, sshape],
        scratch_shapes=[pltpu.VMEM((l, GLA_VW), F32),
                        pltpu.VMEM((2, 2 * GLA_DV, 2 * GLA_DK), F32),
                        pltpu.VMEM((2, 2 * GLA_DV, 2 * GLA_DK), F32)],
        compiler_params=_cparams(("parallel", "arbitrary")),
        name="gla_scan",
    )(q, k, v, gf, gb, q, k, v, gb, gate, gain, s0f, s0b, nmain, nbwd, lmask)


def _outproj_kernel(gla_ref, yc_ref, x_ref, gt_ref, gp_ref, w_ref, o_ref):
    y = (jnp.dot(gla_ref[...], w_ref[:GLA_VW, :], preferred_element_type=F32)
         + jnp.dot(yc_ref[...], w_ref[GLA_VW:, :], preferred_element_type=F32))
    o_ref[...] = x_ref[...] + gt_ref[...] * _rms(y, gp_ref[...])


def _out_project(gla, yc, x2, mod3, mod_row, gpost, w_out, *, tm):
    n, d = x2.shape
    tile = lambda w: pl.BlockSpec((tm, w), lambda i: (i, 0))
    const = lambda a: pl.BlockSpec(a.shape, lambda i: (0,) * a.ndim)
    return pl.pallas_call(
        _outproj_kernel,
        grid=(n // tm,),
        in_specs=[tile(GLA_VW), tile(CONV_CH), tile(d),
                  pl.BlockSpec((None, 1, d), lambda i: (mod_row(i), 0, 2)), const(gpost), const(w_out)],
        out_specs=tile(d),
        out_shape=jax.ShapeDtypeStruct((n, d), F32),
        compiler_params=_cparams(("parallel",)),
        name="out_proj",
    )(gla, yc, x2, mod3, gpost, w_out)


def _dense_kernel(x_ref, sh_ref, sc_ref, gt_ref, gpre_ref, gpost_ref, w1_ref, w3_ref, w2_ref,
                  o_ref, hb_ref, acc_ref):
    f = pl.program_id(1)

    @pl.when(f == 0)
    def _():
        h = _rms(x_ref[...], gpre_ref[...]) * (1.0 + sc_ref[...]) + sh_ref[...]
        hb_ref[...] = h.astype(BF16)
        acc_ref[...] = jnp.zeros_like(acc_ref)

    hb = hb_ref[...]
    a = jnp.dot(hb, w1_ref[...], preferred_element_type=F32)
    b = jnp.dot(hb, w3_ref[...], preferred_element_type=F32)
    acc_ref[...] += jnp.dot((_silu(a) * b).astype(BF16), w2_ref[...], preferred_element_type=F32)

    @pl.when(f == pl.num_programs(1) - 1)
    def _():
        o_ref[...] = x_ref[...] + gt_ref[...] * _rms(acc_ref[...], gpost_ref[...])


def _dense_mix(x2, mod3, mod_row, gpre, gpost, w1, w3, w2, *, tm, tf):
    n, d = x2.shape
    ff = w1.shape[1]
    row = lambda c: pl.BlockSpec((None, 1, d), lambda i, f: (mod_row(i), 0, c))
    const = lambda a: pl.BlockSpec(a.shape, lambda i, f: (0,) * a.ndim)
    tile = pl.BlockSpec((tm, d), lambda i, f: (i, 0))
    return pl.pallas_call(
        _dense_kernel,
        grid=(n // tm, ff // tf),
        in_specs=[tile, row(3), row(4), row(5), const(gpre), const(gpost),
                  pl.BlockSpec((d, tf), lambda i, f: (0, f)),
                  pl.BlockSpec((d, tf), lambda i, f: (0, f)),
                  pl.BlockSpec((tf, d), lambda i, f: (f, 0))],
        out_specs=tile,
        out_shape=jax.ShapeDtypeStruct((n, d), F32),
        scratch_shapes=[pltpu.VMEM((tm, d), BF16), pltpu.VMEM((tm, d), F32)],
        compiler_params=_cparams(("parallel", "arbitrary")),
        name="dense_mix",
    )(x2, mod3, mod3, mod3, gpre, gpost, w1, w3, w2)


HALF = D_MODEL // 2
HI_MASK = 0xFFFF0000


def _pack_rows(v):
    bits = pltpu.bitcast(v.astype(BF16).astype(F32), jnp.uint32)
    return (bits[:, :HALF] >> 16) | (bits[:, HALF:] & jnp.uint32(HI_MASK))


def _unpack_rows(w):
    lo = pltpu.bitcast(w << 16, F32)
    hi = pltpu.bitcast(w & jnp.uint32(HI_MASK), F32)
    return jnp.concatenate([lo, hi], axis=1)


def _route_kernel(x_ref, sh_ref, sc_ref, gpre_ref, wr_ref, tri_ref, hp_ref, info_ref, cnt_ref, carry_ref):
    i = pl.program_id(0)

    @pl.when(i == 0)
    def _():
        carry_ref[...] = jnp.zeros_like(carry_ref)

    h = _rms(x_ref[...], gpre_ref[...]) * (1.0 + sc_ref[...]) + sh_ref[...]
    hp_ref[...] = _pack_rows(h)
    logits = jnp.dot(h, wr_ref[...], precision=HIGHEST, preferred_element_type=F32)
    lane = lax.broadcasted_iota(jnp.int32, logits.shape, 1).astype(F32)
    logits = jnp.where(lane < N_EXPERTS, logits, -jnp.inf)
    m1 = jnp.max(logits, axis=-1, keepdims=True)
    i1 = jnp.min(jnp.where(logits == m1, lane, float(LANES)), axis=-1, keepdims=True)
    rest = jnp.where(lane == i1, -jnp.inf, logits)
    m2 = jnp.max(rest, axis=-1, keepdims=True)
    i2 = jnp.min(jnp.where(rest == m2, lane, float(LANES)), axis=-1, keepdims=True)
    e2 = jnp.exp(m2 - m1)
    den = 1.0 + e2
    pick = jnp.where((lane == i1) | (lane == i2), 1.0, 0.0)
    rank = jnp.dot(tri_ref[...], pick.astype(BF16), preferred_element_type=F32) + carry_ref[...]
    carry_ref[...] += jnp.sum(pick, axis=0, keepdims=True)
    cnt_ref[...] = carry_ref[...]
    r1 = jnp.sum(jnp.where(lane == i1, rank, 0.0), axis=-1, keepdims=True)
    r2 = jnp.sum(jnp.where(lane == i2, rank, 0.0), axis=-1, keepdims=True)
    info = jnp.zeros_like(logits)
    for col, val in enumerate((i1, i2, 1.0 / den, e2 / den, r1, r2)):
        info = jnp.where(lane == col, val, info)
    info_ref[...] = info


def _route(x2, mod3, mod_row, gpre, w_r, *, tm):
    n, d = x2.shape
    row = lambda c: pl.BlockSpec((None, 1, d), lambda i: (mod_row(i), 0, c))
    const = lambda a: pl.BlockSpec(a.shape, lambda i: (0,) * a.ndim)
    tri = jnp.asarray(np.tril(np.ones((tm, tm), np.float32), -1), BF16)
    return pl.pallas_call(
        _route_kernel,
        grid=(n // tm,),
        in_specs=[pl.BlockSpec((tm, d), lambda i: (i, 0)), row(3), row(4), const(gpre), const(w_r), const(tri)],
        out_specs=[pl.BlockSpec((tm, HALF), lambda i: (i, 0)), pl.BlockSpec((tm, LANES), lambda i: (i, 0)),
                   pl.BlockSpec((1, LANES), lambda i: (0, 0))],
        out_shape=[jax.ShapeDtypeStruct((n, HALF), jnp.uint32), jax.ShapeDtypeStruct((n, LANES), F32),
                   jax.ShapeDtypeStruct((1, LANES), F32)],
        scratch_shapes=[pltpu.VMEM((1, LANES), F32)],
        compiler_params=_cparams(("arbitrary",)),
        name="moe_route",
    )(x2, mod3, mod3, gpre, w_r, tri)


def _row_copy(src, s, dst, t, sem):
    return pltpu.make_async_copy(src.at[pl.ds(s, 1), :], dst.at[pl.ds(t, 1), :], sem)


def _dispatch_kernel(pos_ref, hp_ref, xs_in_ref, xs_ref, sem, *, tm):
    del xs_in_ref
    t0 = pl.program_id(0) * tm

    def start(t, carry):
        _row_copy(hp_ref, t0 + t, xs_ref, pos_ref[0, 2 * t], sem).start()
        _row_copy(hp_ref, t0 + t, xs_ref, pos_ref[0, 2 * t + 1], sem).start()
        return carry

    def wait(t, carry):
        _row_copy(hp_ref, 0, xs_ref, 0, sem).wait()
        return carry

    lax.fori_loop(0, tm, start, 0)
    lax.fori_loop(0, 2 * tm, wait, 0)


def _dispatch(pos3, hp, n_sorted, *, tm):
    n = hp.shape[0]
    xs0 = jnp.zeros((n_sorted, HALF), jnp.uint32)
    return pl.pallas_call(
        functools.partial(_dispatch_kernel, tm=tm),
        grid=(n // tm,),
        in_specs=[pl.BlockSpec((None, 1, 2 * tm), lambda i: (i, 0, 0), memory_space=pltpu.SMEM),
                  pl.BlockSpec(memory_space=pl.ANY), pl.BlockSpec(memory_space=pl.ANY)],
        out_specs=pl.BlockSpec(memory_space=pl.ANY),
        out_shape=jax.ShapeDtypeStruct((n_sorted, HALF), jnp.uint32),
        scratch_shapes=[pltpu.SemaphoreType.DMA],
        input_output_aliases={2: 0},
        compiler_params=pltpu.CompilerParams(dimension_semantics=("arbitrary",), has_side_effects=True),
        name="moe_dispatch",
    )(pos3, hp, xs0)


def _expert_kernel(te_ref, nu_ref, xs_ref, w1_ref, w3_ref, w2_ref, ys_ref, *, tf):
    del te_ref

    @pl.when(pl.program_id(0) < nu_ref[0])
    def _():
        xw = xs_ref[...]
        lo = pltpu.bitcast(xw << 16, F32).astype(BF16)
        hi = pltpu.bitcast(xw & jnp.uint32(HI_MASK), F32).astype(BF16)
        acc = jnp.zeros((xw.shape[0], D_MODEL), F32)
        for f0 in range(0, w1_ref.shape[1], tf):
            cols = slice(f0, f0 + tf)
            a = (jnp.dot(lo, w1_ref[:HALF, cols], preferred_element_type=F32)
                 + jnp.dot(hi, w1_ref[HALF:, cols], preferred_element_type=F32))
            b = (jnp.dot(lo, w3_ref[:HALF, cols], preferred_element_type=F32)
                 + jnp.dot(hi, w3_ref[HALF:, cols], preferred_element_type=F32))
            acc = acc + jnp.dot((_silu(a) * b).astype(BF16), w2_ref[cols, :], preferred_element_type=F32)
        ys_ref[...] = _pack_rows(acc)


def _experts(tile_expert, n_used, xs, w1, w3, w2, *, tmg, tf):
    n_sorted = xs.shape[0]
    _, d, ff = w1.shape
    blk = lambda j, te, nu: (jnp.minimum(j, nu[0] - 1), 0)
    grid_spec = pltpu.PrefetchScalarGridSpec(
        num_scalar_prefetch=2,
        grid=(n_sorted // tmg,),
        in_specs=[pl.BlockSpec((tmg, HALF), blk),
                  pl.BlockSpec((None, d, ff), lambda j, te, nu: (te[j], 0, 0), pipeline_mode=pl.Buffered(1)),
                  pl.BlockSpec((None, d, ff), lambda j, te, nu: (te[j], 0, 0), pipeline_mode=pl.Buffered(1)),
                  pl.BlockSpec((None, ff, d), lambda j, te, nu: (te[j], 0, 0), pipeline_mode=pl.Buffered(1))],
        out_specs=pl.BlockSpec((tmg, HALF), blk),
    )
    return pl.pallas_call(
        functools.partial(_expert_kernel, tf=tf),
        grid_spec=grid_spec,
        out_shape=jax.ShapeDtypeStruct((n_sorted, HALF), jnp.uint32),
        compiler_params=_cparams(("arbitrary",)),
        name="moe_experts",
    )(tile_expert, n_used, xs, w1, w3, w2)


def _combine_kernel(pos_ref, x_ref, gt_ref, gpost_ref, info_ref, ys_ref, o_ref, buf_ref, sem, *, tm):
    def start(t, carry):
        _row_copy(ys_ref, pos_ref[0, 2 * t], buf_ref.at[0], t, sem).start()
        _row_copy(ys_ref, pos_ref[0, 2 * t + 1], buf_ref.at[1], t, sem).start()
        return carry

    def wait(t, carry):
        _row_copy(ys_ref, 0, buf_ref.at[0], 0, sem).wait()
        return carry

    lax.fori_loop(0, tm, start, 0)
    lax.fori_loop(0, 2 * tm, wait, 0)
    info = info_ref[...]
    lane = lax.broadcasted_iota(jnp.int32, info.shape, 1)
    wt1 = jnp.sum(jnp.where(lane == 2, info, 0.0), axis=-1, keepdims=True)
    wt2 = jnp.sum(jnp.where(lane == 3, info, 0.0), axis=-1, keepdims=True)
    y = wt1 * _unpack_rows(buf_ref[0]) + wt2 * _unpack_rows(buf_ref[1])
    o_ref[...] = x_ref[...] + gt_ref[...] * _rms(y, gpost_ref[...])


def _combine(pos3, x2, mod3, mod_row, gpost, info, ys, *, tm):
    n, d = x2.shape
    tile = pl.BlockSpec((tm, d), lambda i: (i, 0))
    return pl.pallas_call(
        functools.partial(_combine_kernel, tm=tm),
        grid=(n // tm,),
        in_specs=[pl.BlockSpec((None, 1, 2 * tm), lambda i: (i, 0, 0), memory_space=pltpu.SMEM),
                  tile, pl.BlockSpec((None, 1, d), lambda i: (mod_row(i), 0, 5)),
                  pl.BlockSpec(gpost.shape, lambda i: (0, 0)),
                  pl.BlockSpec((tm, LANES), lambda i: (i, 0)), pl.BlockSpec(memory_space=pl.ANY)],
        out_specs=tile,
        out_shape=jax.ShapeDtypeStruct((n, d), F32),
        scratch_shapes=[pltpu.VMEM((2, tm, HALF), jnp.uint32), pltpu.SemaphoreType.DMA],
        compiler_params=_cparams(("arbitrary",)),
        name="moe_combine",
    )(pos3, x2, mod3, gpost, info, ys)


def _moe_mix(x2, mod3, mod_row, gpre, gpost, w_r, w1, w3, w2, *, tm, tmg, tf):
    n = x2.shape[0]
    hp, info, cnt = _route(x2, mod3, mod_row, gpre, w_r, tm=tm)
    counts = cnt[0, :N_EXPERTS].astype(jnp.int32)
    padded = ((counts + tmg - 1) // tmg) * tmg
    ends = jnp.cumsum(padded)
    starts = ends - padded
    n_tiles = (2 * n) // tmg + N_EXPERTS
    picks = info[:, 0:2].astype(jnp.int32)
    pos = starts[picks] + info[:, 4:6].astype(jnp.int32)
    pos3 = pos.reshape(n // tm, 1, 2 * tm)
    tile_expert = jnp.sum(jnp.arange(n_tiles, dtype=jnp.int32)[:, None] * tmg >= ends[None, :], axis=1)
    n_used = (ends[-1] // tmg).astype(jnp.int32).reshape(1)
    tile_expert = jnp.minimum(tile_expert, tile_expert[jnp.maximum(n_used[0] - 1, 0)]).astype(jnp.int32)
    xs = _dispatch(pos3, hp, n_tiles * tmg, tm=tm)
    ys = _experts(tile_expert, n_used, xs, w1, w3, w2, tmg=tmg, tf=tf)
    return _combine(pos3, x2, mod3, mod_row, gpost, info, ys, tm=tm)


def kernel(x, c, ctx, c_ctx, w_mod, b_mod, g_mix_pre, g_mix_post, w_in, w_decay, b_decay, gla_norm,
           conv_w, w_out, g_ffn_pre, g_ffn_post, w1, w3, w2, w_router, e_w1, e_w3, e_w2):
    bsz, seq, d = x.shape
    ctx_len = ctx.shape[1]
    depth = w_mod.shape[0]
    assert d == D_MODEL and seq % (4 * CHUNK) == 0 and ctx_len % CHUNK == 0 and bsz + 1 <= MOD_ROWS

    c_all = jnp.zeros((MOD_ROWS, d), F32).at[:bsz].set(c).at[bsz].set(c_ctx)
    mod = _modulation(c_all, w_mod, b_mod)
    consts = _gla_constants()

    tm_x = 512
    tm_c = ctx_len
    x_row = lambda i: i // (seq // tm_x)
    c_row = lambda i: bsz
    x2 = x.reshape(bsz * seq, d)
    xc2 = ctx.reshape(bsz * ctx_len, d)
    zero_state = jnp.zeros((bsz, 2, 2 * GLA_DV, 2 * GLA_DK), F32)
    row2 = lambda a: a.reshape(1, -1)

    o_q, o_k, o_v = 0, GLA_KW, 2 * GLA_KW
    o_g = o_v + GLA_VW
    o_a = o_g + GLA_VW
    o_c = o_a + 2 * DECAY_RANK

    for i in range(depth):
        last = i == depth - 1
        mod3 = mod[i].reshape(MOD_ROWS, 1, 6 * d)
        wi = w_in[i]
        wm = wi[:, :o_a].astype(BF16)
        wa = jnp.zeros((d, LANES), F32).at[:, :2 * DECAY_RANK].set(wi[:, o_a:o_c]).astype(BF16)
        wc = wi[:, o_c:].astype(BF16)
        wd = jnp.zeros((LANES, 2 * GLA_KW), F32)
        wd = wd.at[:DECAY_RANK, :GLA_KW].set(w_decay[i, 0]).at[DECAY_RANK:2 * DECAY_RANK, GLA_KW:].set(w_decay[i, 1])
        bd = b_decay[i].reshape(1, 2 * GLA_KW)
        wo = w_out[i].astype(BF16)
        gain = row2(gla_norm[i])

        def mix(tokens, mod_row, tm, seg, nseq, slen, s0f, s0b, cps):
            q, k, v, gate, gf, gb, yc = _project(tokens, mod3, mod_row, row2(g_mix_pre[i]), wm, wa, wc, wd, bd,
                                                 conv_w[i], tm=tm, seg=seg)
            r3 = lambda a: a.reshape(nseq, slen, a.shape[-1])
            o, sf, sb = _gla(r3(q), r3(k), r3(v), r3(gf), r3(gb), r3(gate), gain, s0f, s0b, consts, cps=cps)
            return o.reshape(nseq * slen, GLA_VW), yc, sf, sb

        o_c_, yc_c, s_f, s_b = mix(xc2, c_row, tm_c, ctx_len, bsz, ctx_len, zero_state, zero_state,
                                   ctx_len // CHUNK)
        o_x, yc_x, _, _ = mix(x2, x_row, tm_x, GRID_W, bsz, seq, s_f, s_b, 4)
        x2 = _out_project(o_x, yc_x, x2, mod3, x_row, row2(g_mix_post[i]), wo, tm=tm_x)
        if not last:
            xc2 = _out_project(o_c_, yc_c, xc2, mod3, c_row, row2(g_mix_post[i]), wo, tm=tm_c)

        j = i // 2
        gpre, gpost = row2(g_ffn_pre[i]), row2(g_ffn_post[i])
        if i % 2 == 0:
            ffn = functools.partial(_dense_mix, gpre=gpre, gpost=gpost, w1=w1[j].astype(BF16),
                                    w3=w3[j].astype(BF16), w2=w2[j].astype(BF16), tf=1408)
        else:
            w_r = jnp.zeros((d, LANES), F32).at[:, :N_EXPERTS].set(w_router[j])
            ffn = functools.partial(_moe_mix, gpre=gpre, gpost=gpost, w_r=w_r, w1=e_w1[j].astype(BF16),
                                    w3=e_w3[j].astype(BF16), w2=e_w2[j].astype(BF16), tmg=512, tf=896)
        x2 = ffn(x2, mod3, x_row, tm=tm_x)
        if not last:
            xc2 = ffn(xc2, mod3, c_row, tm=tm_c)
    return x2.reshape(bsz, seq, d)
```

```python
import functools

import numpy as np
import jax
import jax.numpy as jnp
from jax import lax
from jax.experimental import pallas as pl
from jax.experimental.pallas import tpu as pltpu

F32 = jnp.float32
BF16 = jnp.bfloat16
HIGHEST = lax.Precision.HIGHEST

D_MODEL = 1024
GLA_HEADS = 4
GLA_DK = 64
GLA_DV = 128
GLA_KW = GLA_HEADS * GLA_DK
GLA_VW = GLA_HEADS * GLA_DV
DECAY_RANK = 16
GATE_NORMALIZER = 16.0
CHUNK = 64
CONV_CH = D_MODEL - GLA_VW
GRID_W = 64
N_EXPERTS = 8
EPS = 1e-6
LANES = 128
N_LEVELS = 6
MM_LEVELS = N_LEVELS - 1
MOD_ROWS = 24
VMEM_LIMIT = 52 * 1024 * 1024


def _cparams(sem):
    return pltpu.CompilerParams(dimension_semantics=sem, vmem_limit_bytes=VMEM_LIMIT)


def _rms(x, gain):
    return x * lax.rsqrt(jnp.mean(x * x, axis=-1, keepdims=True) + EPS) * gain


def _silu(x):
    return x / (1.0 + jnp.exp(-x))


def _mod_kernel(c_ref, w_ref, b_ref, o_ref):
    s = _silu(c_ref[...])
    o_ref[...] = jnp.dot(s, w_ref[...], precision=HIGHEST, preferred_element_type=F32) + b_ref[...]


def _modulation(c_all, w_mod, b_mod):
    depth, d, n = w_mod.shape
    tn = 1536
    return pl.pallas_call(
        _mod_kernel,
        grid=(depth, n // tn),
        in_specs=[
            pl.BlockSpec((MOD_ROWS, d), lambda i, j: (0, 0)),
            pl.BlockSpec((None, d, tn), lambda i, j: (i, 0, j)),
            pl.BlockSpec((None, 1, tn), lambda i, j: (i, 0, j)),
        ],
        out_specs=pl.BlockSpec((None, MOD_ROWS, tn), lambda i, j: (i, 0, j)),
        out_shape=jax.ShapeDtypeStruct((depth, MOD_ROWS, n), F32),
        compiler_params=_cparams(("parallel", "parallel")),
        name="modulation",
    )(c_all, w_mod, b_mod.reshape(depth, 1, n))


def _proj_kernel(x_ref, sh_ref, sc_ref, g_ref, wm_ref, wa_ref, wc_ref, wd_ref, bd_ref, cw_ref,
                 q_ref, k_ref, v_ref, gate_ref, gf_ref, gb_ref, yc_ref, *, seg):
    h = _rms(x_ref[...], g_ref[...]) * (1.0 + sc_ref[...]) + sh_ref[...]
    hb = h.astype(BF16)
    p = jnp.dot(hb, wm_ref[...], preferred_element_type=F32)
    q_ref[...] = (p[:, :GLA_KW] * (GLA_DK ** -0.5)).astype(BF16)
    k_ref[...] = p[:, GLA_KW:2 * GLA_KW].astype(BF16)
    v_ref[...] = p[:, 2 * GLA_KW:2 * GLA_KW + GLA_VW].astype(BF16)
    gate_ref[...] = p[:, 2 * GLA_KW + GLA_VW:].astype(BF16)
    pa = jnp.dot(hb, wa_ref[...], preferred_element_type=F32)
    xd = jnp.dot(pa, wd_ref[...], precision=HIGHEST, preferred_element_type=F32) + bd_ref[...]
    ls = (jnp.minimum(xd, 0.0) - jnp.log1p(jnp.exp(-jnp.abs(xd)))) * (1.0 / GATE_NORMALIZER)
    gf_ref[...] = ls[:, :GLA_KW]
    gb_ref[...] = ls[:, GLA_KW:]
    pc = jnp.dot(hb, wc_ref[...], preferred_element_type=F32)
    u = pc[:, CONV_CH:2 * CONV_CH] * pc[:, 2 * CONV_CH:]
    tm = u.shape[0]
    row = lax.broadcasted_iota(jnp.int32, u.shape, 0) & (seg - 1)
    u_prev = jnp.where(row == 0, 0.0, pltpu.roll(u, 1, 0))
    u_next = jnp.where(row == seg - 1, 0.0, pltpu.roll(u, tm - 1, 0))
    cw = cw_ref[...]
    yc = pc[:, :CONV_CH] * (cw[0:1] * u_prev + cw[1:2] * u + cw[2:3] * u_next)
    yc_ref[...] = yc.astype(BF16)


def _project(x2, mod3, mod_row, gain, wm, wa, wc, wd, bd, cw, *, tm, seg):
    n = x2.shape[0]
    d = D_MODEL
    row = lambda c: pl.BlockSpec((None, 1, d), lambda i: (mod_row(i), 0, c))
    const = lambda a: pl.BlockSpec(a.shape, lambda i: (0,) * a.ndim)
    tile = lambda w: pl.BlockSpec((tm, w), lambda i: (i, 0))
    shapes = [(GLA_KW, BF16), (GLA_KW, BF16), (GLA_VW, BF16), (GLA_VW, BF16),
              (GLA_KW, F32), (GLA_KW, F32), (CONV_CH, BF16)]
    return pl.pallas_call(
        functools.partial(_proj_kernel, seg=seg),
        grid=(n // tm,),
        in_specs=[tile(d), row(0), row(1), const(gain), const(wm), const(wa), const(wc),
                  const(wd), const(bd), const(cw)],
        out_specs=[tile(w) for w, _ in shapes],
        out_shape=[jax.ShapeDtypeStruct((n, w), t) for w, t in shapes],
        compiler_params=_cparams(("parallel",)),
        name="in_proj",
    )(x2, mod3, mod3, gain, wm, wa, wc, wd, bd, cw)


def _gla_constants():
    c = CHUNK
    main = np.zeros((2 * MM_LEVELS + 2, c, 2 * c), np.float32)
    lmask = np.zeros((N_LEVELS + 1, c, c), np.float32)
    for lvl in range(N_LEVELS):
        s = c >> (lvl + 1)
        for i in range(c):
            mid = (i // (2 * s)) * 2 * s + s
            if lvl < MM_LEVELS and i >= mid:
                main[2 * lvl, i, mid:i + 1] = 1.0
                main[2 * lvl + 1, i, c + mid:c + i] = 1.0
            elif lvl < MM_LEVELS:
                main[2 * lvl, i, c + i:c + mid] = 1.0
                main[2 * lvl + 1, i, i + 1:mid] = 1.0
            for j in range(c):
                same = (j // (2 * s)) == (i // (2 * s))
                lmask[lvl, i, j] = float(same and ((i >= mid) != (j >= mid)))
    lmask[N_LEVELS] = 2.0 * np.eye(c)
    bwd = np.zeros((2, c, c), np.float32)
    for i in range(c):
        main[2 * MM_LEVELS, i, :i + 1] = 1.0
        main[2 * MM_LEVELS + 1, i, i + 1:c] = 1.0
        bwd[0, i, i:] = 1.0
        bwd[1, i, :i] = 1.0
    main = main.reshape(-1, 2 * c)
    bwd = bwd.reshape(-1, c)
    lmask = np.concatenate([lmask, lmask], axis=-1)
    return jnp.asarray(main, BF16), jnp.asarray(bwd, BF16), jnp.asarray(lmask, F32)


def _sum_dot(n_ref, g, parts):
    gb = g.astype(BF16)
    rows = n_ref.shape[0] // parts
    outs = [jnp.dot(n_ref[p * rows:(p + 1) * rows, :], gb, preferred_element_type=F32) for p in range(parts)]
    per = rows // CHUNK
    return lambda b: outs[b // per][(b % per) * CHUNK:(b % per + 1) * CHUNK]


_NT = (((1,), (1,)), ((), ()))
_TN = (((0,), (0,)), ((), ()))


def _gla_kernel(qf_ref, kf_ref, vf_ref, gff_ref, gbf_ref, qb_ref, kb_ref, vb_ref, gbb_ref,
                gate_ref, gain_ref, s0f_ref, s0b_ref, nmain_ref, nbwd_ref, lmask_ref,
                out_ref, sf_out_ref, sb_out_ref, acc_ref, sf_ref, sb_ref, *, cps):
    n = pl.program_id(1)
    nb = pl.num_programs(1)
    c = CHUNK
    pw = 2 * GLA_DK
    vw = 2 * GLA_DV

    @pl.when(n == 0)
    def _():
        acc_ref[...] = jnp.zeros_like(acc_ref)
        sf_ref[...] = s0f_ref[...]
        sb_ref[...] = s0b_ref[...]

    klane = lax.broadcasted_iota(jnp.int32, (c, GLA_KW), 1)
    k_even = (klane & (pw - 1)) < GLA_DK
    vlane = lax.broadcasted_iota(jnp.int32, (c, vw), 1)
    v_low = vlane < GLA_DV
    srow = lax.broadcasted_iota(jnp.int32, (vw, pw), 0)
    scol = lax.broadcasted_iota(jnp.int32, (vw, pw), 1)
    s_diag = (srow < GLA_DV) == (scol < GLA_DK)
    zero_b = jnp.zeros((), BF16)

    def inter(qi, ki, dec, v, s_ref):
        outs = []
        for p in range(2):
            st = s_ref[p]
            outs.append(lax.dot_general(qi[:, p * pw:(p + 1) * pw], st.astype(BF16), _NT,
                                        preferred_element_type=F32))
            upd = lax.dot_general(v[:, p * vw:(p + 1) * vw], ki[:, p * pw:(p + 1) * pw], _TN,
                                  preferred_element_type=F32)
            s_ref[p] = st * dec[:, p * pw:(p + 1) * pw] + jnp.where(s_diag, upd, 0.0)
        return jnp.concatenate(outs, axis=1)

    odd_row = (lax.broadcasted_iota(jnp.int32, (c, GLA_KW), 0) & 1) == 1

    def fwd_chunk(j):
        r0 = pl.multiple_of(j * c, c)
        rows = pl.ds(r0, c)
        qb16 = qf_ref[rows, :]
        kb16 = kf_ref[rows, :]
        v = vf_ref[rows, :]
        q = qb16.astype(F32)
        k = kb16.astype(F32)
        gf = gff_ref[rows, :]
        gb = gbf_ref[rows, :]
        e = _sum_dot(nmain_ref, jnp.concatenate([gf, gb], axis=0), 2)
        att = [jnp.zeros((c, pw), F32), jnp.zeros((c, pw), F32)]
        for lvl in range(N_LEVELS + 1):
            if lvl < MM_LEVELS:
                qt = (q * jnp.exp(e(2 * lvl))).astype(BF16)
                kt = (k * jnp.exp(e(2 * lvl + 1))).astype(BF16)
            elif lvl < N_LEVELS:
                qt, kt = (q * jnp.exp(jnp.where(odd_row, gf, gb))).astype(BF16), kb16
            else:
                qt, kt = qb16, kb16
            ke = jnp.where(k_even, kt, zero_b)
            ko = jnp.where(k_even, zero_b, kt)
            m = lmask_ref[lvl]
            for p in range(2):
                kbd = jnp.concatenate([ke[:, p * pw:(p + 1) * pw], ko[:, p * pw:(p + 1) * pw]], axis=0)
                att[p] = att[p] + m * lax.dot_general(qt[:, p * pw:(p + 1) * pw], kbd, _NT,
                                                      preferred_element_type=F32)
        outs = []
        for p in range(2):
            vp = v[:, p * vw:(p + 1) * vw]
            vbd = jnp.concatenate([jnp.where(v_low, vp, zero_b), jnp.where(v_low, zero_b, vp)], axis=0)
            outs.append(jnp.dot(att[p].astype(BF16), vbd, preferred_element_type=F32))
        o = jnp.concatenate(outs, axis=1)
        cum = e(2 * MM_LEVELS)
        qi = (q * jnp.exp(cum)).astype(BF16)
        ki = (k * jnp.exp(e(2 * MM_LEVELS + 1))).astype(BF16)
        o = o + inter(qi, ki, jnp.exp(cum[c - 1:c]), v, sf_ref)
        a0 = pl.multiple_of(n * (cps * c) + r0, c)
        acc_ref[pl.ds(a0, c), :] += o

    def bwd_chunk(j):
        r0 = pl.multiple_of((cps - 1 - j) * c, c)
        rows = pl.ds(r0, c)
        q = qb_ref[rows, :].astype(F32)
        k = kb_ref[rows, :].astype(F32)
        v = vb_ref[rows, :]
        e = _sum_dot(nbwd_ref, gbb_ref[rows, :], 1)
        cum = e(0)
        qi = (q * jnp.exp(cum)).astype(BF16)
        ki = (k * jnp.exp(e(1))).astype(BF16)
        o = inter(qi, ki, jnp.exp(cum[0:1]), v, sb_ref)
        a0 = pl.multiple_of((nb - 1 - n) * (cps * c) + r0, c)
        acc_ref[pl.ds(a0, c), :] += o

    def both(j, carry):
        fwd_chunk(j)
        bwd_chunk(j)
        return carry

    lax.fori_loop(0, cps, both, 0, unroll=True)

    @pl.when(n == nb - 1)
    def _():
        sf_out_ref[...] = sf_ref[...]
        sb_out_ref[...] = sb_ref[...]
        gain = gain_ref[...]
        rb = 256

        def gate_rows(i, carry):
            rows = pl.ds(pl.multiple_of(i * rb, rb), rb)
            o = acc_ref[rows, :]
            gt = gate_ref[rows, :].astype(F32)
            normed = jnp.concatenate(
                [_rms(o[:, h * GLA_DV:(h + 1) * GLA_DV], gain) for h in range(GLA_HEADS)], axis=1)
            out_ref[rows, :] = (normed * _silu(gt)).astype(BF16)
            return carry

        lax.fori_loop(0, acc_ref.shape[0] // rb, gate_rows, 0)


def _gla(q, k, v, gf, gb, gate, gain, s0f, s0b, consts, *, cps):
    b, l, _ = q.shape
    t = cps * CHUNK
    nb = l // t
    nmain, nbwd, lmask = consts
    fwd = lambda w: pl.BlockSpec((None, t, w), lambda i, n: (i, n, 0))
    bwd = lambda w: pl.BlockSpec((None, t, w), lambda i, n: (i, nb - 1 - n, 0))
    whole = lambda w: pl.BlockSpec((None, l, w), lambda i, n: (i, 0, 0))
    state = pl.BlockSpec((None, 2, 2 * GLA_DV, 2 * GLA_DK), lambda i, n: (i, 0, 0, 0))
    const = lambda a: pl.BlockSpec(a.shape, lambda i, n: (0,) * a.ndim)
    sshape = jax.ShapeDtypeStruct((b, 2, 2 * GLA_DV, 2 * GLA_DK), F32)
    return pl.pallas_call(
        functools.partial(_gla_kernel, cps=cps),
        grid=(b, nb),
        in_specs=[fwd(GLA_KW), fwd(GLA_KW), fwd(GLA_VW), fwd(GLA_KW), fwd(GLA_KW),
                  bwd(GLA_KW), bwd(GLA_KW), bwd(GLA_VW), bwd(GLA_KW),
                  whole(GLA_VW), const(gain), state, state, const(nmain), const(nbwd), const(lmask)],
        out_specs=[whole(GLA_VW), state, state],
        out_shape=[jax.ShapeDtypeStruct((b, l, GLA_VW), BF16), sshape, sshape],
        scratch_shapes=[pltpu.VMEM((l, GLA_VW), F32),
                        pltpu.VMEM((2, 2 * GLA_DV, 2 * GLA_DK), F32),
                        pltpu.VMEM((2, 2 * GLA_DV, 2 * GLA_DK), F32)],
        compiler_params=_cparams(("parallel", "arbitrary")),
        name="gla_scan",
    )(q, k, v, gf, gb, q, k, v, gb, gate, gain, s0f, s0b, nmain, nbwd, lmask)


def _outproj_kernel(gla_ref, yc_ref, x_ref, gt_ref, gp_ref, w_ref, o_ref):
    y = (jnp.dot(gla_ref[...], w_ref[:GLA_VW, :], preferred_element_type=F32)
         + jnp.dot(yc_ref[...], w_ref[GLA_VW:, :], preferred_element_type=F32))
    o_ref[...] = x_ref[...] + gt_ref[...] * _rms(y, gp_ref[...])


def _out_project(gla, yc, x2, mod3, mod_row, gpost, w_out, *, tm):
    n, d = x2.shape
    tile = lambda w: pl.BlockSpec((tm, w), lambda i: (i, 0))
    const = lambda a: pl.BlockSpec(a.shape, lambda i: (0,) * a.ndim)
    return pl.pallas_call(
        _outproj_kernel,
        grid=(n // tm,),
        in_specs=[tile(GLA_VW), tile(CONV_CH), tile(d),
                  pl.BlockSpec((None, 1, d), lambda i: (mod_row(i), 0, 2)), const(gpost), const(w_out)],
        out_specs=tile(d),
        out_shape=jax.ShapeDtypeStruct((n, d), F32),
        compiler_params=_cparams(("parallel",)),
        name="out_proj",
    )(gla, yc, x2, mod3, gpost, w_out)


def _dense_kernel(x_ref, sh_ref, sc_ref, gt_ref, gpre_ref, gpost_ref, w1_ref, w3_ref, w2_ref,
                  o_ref, hb_ref, acc_ref):
    f = pl.program_id(1)

    @pl.when(f == 0)
    def _():
        h = _rms(x_ref[...], gpre_ref[...]) * (1.0 + sc_ref[...]) + sh_ref[...]
        hb_ref[...] = h.astype(BF16)
        acc_ref[...] = jnp.zeros_like(acc_ref)

    hb = hb_ref[...]
    a = jnp.dot(hb, w1_ref[...], preferred_element_type=F32)
    b = jnp.dot(hb, w3_ref[...], preferred_element_type=F32)
    acc_ref[...] += jnp.dot((_silu(a) * b).astype(BF16), w2_ref[...], preferred_element_type=F32)

    @pl.when(f == pl.num_programs(1) - 1)
    def _():
        o_ref[...] = x_ref[...] + gt_ref[...] * _rms(acc_ref[...], gpost_ref[...])


def _dense_mix(x2, mod3, mod_row, gpre, gpost, w1, w3, w2, *, tm, tf):
    n, d = x2.shape
    ff = w1.shape[1]
    row = lambda c: pl.BlockSpec((None, 1, d), lambda i, f: (mod_row(i), 0, c))
    const = lambda a: pl.BlockSpec(a.shape, lambda i, f: (0,) * a.ndim)
    tile = pl.BlockSpec((tm, d), lambda i, f: (i, 0))
    return pl.pallas_call(
        _dense_kernel,
        grid=(n // tm, ff // tf),
        in_specs=[tile, row(3), row(4), row(5), const(gpre), const(gpost),
                  pl.BlockSpec((d, tf), lambda i, f: (0, f)),
                  pl.BlockSpec((d, tf), lambda i, f: (0, f)),
                  pl.BlockSpec((tf, d), lambda i, f: (f, 0))],
        out_specs=tile,
        out_shape=jax.ShapeDtypeStruct((n, d), F32),
        scratch_shapes=[pltpu.VMEM((tm, d), BF16), pltpu.VMEM((tm, d), F32)],
        compiler_params=_cparams(("parallel", "arbitrary")),
        name="dense_mix",
    )(x2, mod3, mod3, mod3, gpre, gpost, w1, w3, w2)


HALF = D_MODEL // 2
HI_MASK = 0xFFFF0000


def _pack_rows(v):
    bits = pltpu.bitcast(v.astype(BF16).astype(F32), jnp.uint32)
    return (bits[:, :HALF] >> 16) | (bits[:, HALF:] & jnp.uint32(HI_MASK))


def _unpack_rows(w):
    lo = pltpu.bitcast(w << 16, F32)
    hi = pltpu.bitcast(w & jnp.uint32(HI_MASK), F32)
    return jnp.concatenate([lo, hi], axis=1)


def _route_kernel(x_ref, sh_ref, sc_ref, gpre_ref, wr_ref, tri_ref, hp_ref, info_ref, cnt_ref, carry_ref):
    i = pl.program_id(0)

    @pl.when(i == 0)
    def _():
        carry_ref[...] = jnp.zeros_like(carry_ref)

    h = _rms(x_ref[...], gpre_ref[...]) * (1.0 + sc_ref[...]) + sh_ref[...]
    hp_ref[...] = _pack_rows(h)
    logits = jnp.dot(h, wr_ref[...], precision=HIGHEST, preferred_element_type=F32)
    lane = lax.broadcasted_iota(jnp.int32, logits.shape, 1).astype(F32)
    logits = jnp.where(lane < N_EXPERTS, logits, -jnp.inf)
    m1 = jnp.max(logits, axis=-1, keepdims=True)
    i1 = jnp.min(jnp.where(logits == m1, lane, float(LANES)), axis=-1, keepdims=True)
    rest = jnp.where(lane == i1, -jnp.inf, logits)
    m2 = jnp.max(rest, axis=-1, keepdims=True)
    i2 = jnp.min(jnp.where(rest == m2, lane, float(LANES)), axis=-1, keepdims=True)
    e2 = jnp.exp(m2 - m1)
    den = 1.0 + e2
    pick = jnp.where((lane == i1) | (lane == i2), 1.0, 0.0)
    rank = jnp.dot(tri_ref[...], pick.astype(BF16), preferred_element_type=F32) + carry_ref[...]
    carry_ref[...] += jnp.sum(pick, axis=0, keepdims=True)
    cnt_ref[...] = carry_ref[...]
    r1 = jnp.sum(jnp.where(lane == i1, rank, 0.0), axis=-1, keepdims=True)
    r2 = jnp.sum(jnp.where(lane == i2, rank, 0.0), axis=-1, keepdims=True)
    info = jnp.zeros_like(logits)
    for col, val in enumerate((i1, i2, 1.0 / den, e2 / den, r1, r2)):
        info = jnp.where(lane == col, val, info)
    info_ref[...] = info


def _route(x2, mod3, mod_row, gpre, w_r, *, tm):
    n, d = x2.shape
    row = lambda c: pl.BlockSpec((None, 1, d), lambda i: (mod_row(i), 0, c))
    const = lambda a: pl.BlockSpec(a.shape, lambda i: (0,) * a.ndim)
    tri = jnp.asarray(np.tril(np.ones((tm, tm), np.float32), -1), BF16)
    return pl.pallas_call(
        _route_kernel,
        grid=(n // tm,),
        in_specs=[pl.BlockSpec((tm, d), lambda i: (i, 0)), row(3), row(4), const(gpre), const(w_r), const(tri)],
        out_specs=[pl.BlockSpec((tm, HALF), lambda i: (i, 0)), pl.BlockSpec((tm, LANES), lambda i: (i, 0)),
                   pl.BlockSpec((1, LANES), lambda i: (0, 0))],
        out_shape=[jax.ShapeDtypeStruct((n, HALF), jnp.uint32), jax.ShapeDtypeStruct((n, LANES), F32),
                   jax.ShapeDtypeStruct((1, LANES), F32)],
        scratch_shapes=[pltpu.VMEM((1, LANES), F32)],
        compiler_params=_cparams(("arbitrary",)),
        name="moe_route",
    )(x2, mod3, mod3, gpre, w_r, tri)


def _row_copy(src, s, dst, t, sem):
    return pltpu.make_async_copy(src.at[pl.ds(s, 1), :], dst.at[pl.ds(t, 1), :], sem)


DMA_UNROLL = 8


def _dispatch_kernel(pos_ref, hp_ref, xs_in_ref, xs_ref, sem, *, tm):
    del xs_in_ref

    def start(t, carry):
        _row_copy(hp_ref, t, xs_ref, pos_ref[0, 2 * t], sem).start()
        _row_copy(hp_ref, t, xs_ref, pos_ref[0, 2 * t + 1], sem).start()
        return carry

    def wait(t, carry):
        _row_copy(hp_ref, 0, xs_ref, 0, sem).wait()
        return carry

    lax.fori_loop(0, tm, start, 0, unroll=DMA_UNROLL)
    lax.fori_loop(0, 2 * tm, wait, 0, unroll=DMA_UNROLL)


def _dispatch(pos3, hp, n_sorted, *, tm):
    n = hp.shape[0]
    xs0 = jnp.zeros((n_sorted, HALF), jnp.uint32)
    return pl.pallas_call(
        functools.partial(_dispatch_kernel, tm=tm),
        grid=(n // tm,),
        in_specs=[pl.BlockSpec((None, 1, 2 * tm), lambda i: (i, 0, 0), memory_space=pltpu.SMEM),
                  pl.BlockSpec((tm, HALF), lambda i: (i, 0)), pl.BlockSpec(memory_space=pl.ANY)],
        out_specs=pl.BlockSpec(memory_space=pl.ANY),
        out_shape=jax.ShapeDtypeStruct((n_sorted, HALF), jnp.uint32),
        scratch_shapes=[pltpu.SemaphoreType.DMA],
        input_output_aliases={2: 0},
        compiler_params=pltpu.CompilerParams(dimension_semantics=("arbitrary",), has_side_effects=True),
        name="moe_dispatch",
    )(pos3, hp, xs0)


def _expert_kernel(te_ref, nu_ref, xs_ref, w1_ref, w3_ref, w2_ref, ys_ref, *, tf):
    del te_ref

    @pl.when(pl.program_id(0) < nu_ref[0])
    def _():
        xw = xs_ref[...]
        lo = pltpu.bitcast(xw << 16, F32).astype(BF16)
        hi = pltpu.bitcast(xw & jnp.uint32(HI_MASK), F32).astype(BF16)
        acc = jnp.zeros((xw.shape[0], D_MODEL), F32)
        for f0 in range(0, w1_ref.shape[1], tf):
            cols = slice(f0, f0 + tf)
            a = (jnp.dot(lo, w1_ref[:HALF, cols], preferred_element_type=F32)
                 + jnp.dot(hi, w1_ref[HALF:, cols], preferred_element_type=F32))
            b = (jnp.dot(lo, w3_ref[:HALF, cols], preferred_element_type=F32)
                 + jnp.dot(hi, w3_ref[HALF:, cols], preferred_element_type=F32))
            acc = acc + jnp.dot((_silu(a) * b).astype(BF16), w2_ref[cols, :], preferred_element_type=F32)
        ys_ref[...] = _pack_rows(acc)


def _experts(tile_expert, n_used, xs, w1, w3, w2, *, tmg, tf):
    n_sorted = xs.shape[0]
    _, d, ff = w1.shape
    blk = lambda j, te, nu: (jnp.minimum(j, nu[0] - 1), 0)
    grid_spec = pltpu.PrefetchScalarGridSpec(
        num_scalar_prefetch=2,
        grid=(n_sorted // tmg,),
        in_specs=[pl.BlockSpec((tmg, HALF), blk),
                  pl.BlockSpec((None, d, ff), lambda j, te, nu: (te[j], 0, 0), pipeline_mode=pl.Buffered(1)),
                  pl.BlockSpec((None, d, ff), lambda j, te, nu: (te[j], 0, 0), pipeline_mode=pl.Buffered(1)),
                  pl.BlockSpec((None, ff, d), lambda j, te, nu: (te[j], 0, 0), pipeline_mode=pl.Buffered(1))],
        out_specs=pl.BlockSpec((tmg, HALF), blk),
    )
    return pl.pallas_call(
        functools.partial(_expert_kernel, tf=tf),
        grid_spec=grid_spec,
        out_shape=jax.ShapeDtypeStruct((n_sorted, HALF), jnp.uint32),
        compiler_params=_cparams(("arbitrary",)),
        name="moe_experts",
    )(tile_expert, n_used, xs, w1, w3, w2)


def _combine_kernel(pos_ref, x_ref, gt_ref, gpost_ref, info_ref, ys_ref, o_ref, buf_ref, sems, *, tm):
    i = pl.program_id(0)
    slot = i & 1

    @pl.when(i < pl.num_programs(0) - 1)
    def _():
        def start(t, carry):
            _row_copy(ys_ref, pos_ref[0, 2 * t], buf_ref.at[slot, 0], t, sems.at[slot]).start()
            _row_copy(ys_ref, pos_ref[0, 2 * t + 1], buf_ref.at[slot, 1], t, sems.at[slot]).start()
            return carry

        lax.fori_loop(0, tm, start, 0, unroll=DMA_UNROLL)

    @pl.when(i > 0)
    def _():
        prev = 1 - slot

        def wait(t, carry):
            _row_copy(ys_ref, 0, buf_ref.at[prev, 0], 0, sems.at[prev]).wait()
            return carry

        lax.fori_loop(0, 2 * tm, wait, 0, unroll=DMA_UNROLL)
        info = info_ref[...]
        lane = lax.broadcasted_iota(jnp.int32, info.shape, 1)
        wt1 = jnp.sum(jnp.where(lane == 2, info, 0.0), axis=-1, keepdims=True)
        wt2 = jnp.sum(jnp.where(lane == 3, info, 0.0), axis=-1, keepdims=True)
        y = wt1 * _unpack_rows(buf_ref[prev, 0]) + wt2 * _unpack_rows(buf_ref[prev, 1])
        o_ref[...] = x_ref[...] + gt_ref[...] * _rms(y, gpost_ref[...])


def _combine(pos3, x2, mod3, mod_row, gpost, info, ys, *, tm):
    n, d = x2.shape
    nt = n // tm
    done = lambda i: jnp.maximum(i - 1, 0)
    tile = pl.BlockSpec((tm, d), lambda i: (done(i), 0))
    return pl.pallas_call(
        functools.partial(_combine_kernel, tm=tm),
        grid=(nt + 1,),
        in_specs=[pl.BlockSpec((None, 1, 2 * tm), lambda i: (jnp.minimum(i, nt - 1), 0, 0),
                               memory_space=pltpu.SMEM),
                  tile, pl.BlockSpec((None, 1, d), lambda i: (mod_row(done(i)), 0, 5)),
                  pl.BlockSpec(gpost.shape, lambda i: (0, 0)),
                  pl.BlockSpec((tm, LANES), lambda i: (done(i), 0)), pl.BlockSpec(memory_space=pl.ANY)],
        out_specs=tile,
        out_shape=jax.ShapeDtypeStruct((n, d), F32),
        scratch_shapes=[pltpu.VMEM((2, 2, tm, HALF), jnp.uint32), pltpu.SemaphoreType.DMA((2,))],
        compiler_params=_cparams(("arbitrary",)),
        name="moe_combine",
    )(pos3, x2, mod3, gpost, info, ys)


def _moe_mix(x2, mod3, mod_row, gpre, gpost, w_r, w1, w3, w2, *, tm, tmg, tf):
    n = x2.shape[0]
    hp, info, cnt = _route(x2, mod3, mod_row, gpre, w_r, tm=tm)
    counts = cnt[0, :N_EXPERTS].astype(jnp.int32)
    padded = ((counts + tmg - 1) // tmg) * tmg
    ends = jnp.cumsum(padded)
    starts = ends - padded
    n_tiles = (2 * n) // tmg + N_EXPERTS
    picks = info[:, 0:2].astype(jnp.int32)
    pos = starts[picks] + info[:, 4:6].astype(jnp.int32)
    pos3 = pos.reshape(n // tm, 1, 2 * tm)
    tile_expert = jnp.sum(jnp.arange(n_tiles, dtype=jnp.int32)[:, None] * tmg >= ends[None, :], axis=1)
    n_used = (ends[-1] // tmg).astype(jnp.int32).reshape(1)
    tile_expert = jnp.minimum(tile_expert, tile_expert[jnp.maximum(n_used[0] - 1, 0)]).astype(jnp.int32)
    xs = _dispatch(pos3, hp, n_tiles * tmg, tm=tm)
    ys = _experts(tile_expert, n_used, xs, w1, w3, w2, tmg=tmg, tf=tf)
    return _combine(pos3, x2, mod3, mod_row, gpost, info, ys, tm=tm)


def kernel(x, c, ctx, c_ctx, w_mod, b_mod, g_mix_pre, g_mix_post, w_in, w_decay, b_decay, gla_norm,
           conv_w, w_out, g_ffn_pre, g_ffn_post, w1, w3, w2, w_router, e_w1, e_w3, e_w2):
    bsz, seq, d = x.shape
    ctx_len = ctx.shape[1]
    depth = w_mod.shape[0]
    assert d == D_MODEL and seq % (4 * CHUNK) == 0 and ctx_len % CHUNK == 0 and bsz + 1 <= MOD_ROWS

    c_all = jnp.zeros((MOD_ROWS, d), F32).at[:bsz].set(c).at[bsz].set(c_ctx)
    mod = _modulation(c_all, w_mod, b_mod)
    consts = _gla_constants()

    tm_x = 512
    tm_c = ctx_len
    x_row = lambda i: i // (seq // tm_x)
    c_row = lambda i: bsz
    x2 = x.reshape(bsz * seq, d)
    xc2 = ctx.reshape(bsz * ctx_len, d)
    zero_state = jnp.zeros((bsz, 2, 2 * GLA_DV, 2 * GLA_DK), F32)
    row2 = lambda a: a.reshape(1, -1)

    o_q, o_k, o_v = 0, GLA_KW, 2 * GLA_KW
    o_g = o_v + GLA_VW
    o_a = o_g + GLA_VW
    o_c = o_a + 2 * DECAY_RANK

    for i in range(depth):
        last = i == depth - 1
        mod3 = mod[i].reshape(MOD_ROWS, 1, 6 * d)
        wi = w_in[i]
        wm = wi[:, :o_a].astype(BF16)
        wa = jnp.zeros((d, LANES), F32).at[:, :2 * DECAY_RANK].set(wi[:, o_a:o_c]).astype(BF16)
        wc = wi[:, o_c:].astype(BF16)
        wd = jnp.zeros((LANES, 2 * GLA_KW), F32)
        wd = wd.at[:DECAY_RANK, :GLA_KW].set(w_decay[i, 0]).at[DECAY_RANK:2 * DECAY_RANK, GLA_KW:].set(w_decay[i, 1])
        bd = b_decay[i].reshape(1, 2 * GLA_KW)
        wo = w_out[i].astype(BF16)
        gain = row2(gla_norm[i])

        def mix(tokens, mod_row, tm, seg, nseq, slen, s0f, s0b, cps):
            q, k, v, gate, gf, gb, yc = _project(tokens, mod3, mod_row, row2(g_mix_pre[i]), wm, wa, wc, wd, bd,
                                                 conv_w[i], tm=tm, seg=seg)
            r3 = lambda a: a.reshape(nseq, slen, a.shape[-1])
            o, sf, sb = _gla(r3(q), r3(k), r3(v), r3(gf), r3(gb), r3(gate), gain, s0f, s0b, consts, cps=cps)
            return o.reshape(nseq * slen, GLA_VW), yc, sf, sb

        o_c_, yc_c, s_f, s_b = mix(xc2, c_row, tm_c, ctx_len, bsz, ctx_len, zero_state, zero_state,
                                   ctx_len // CHUNK)
        o_x, yc_x, _, _ = mix(x2, x_row, tm_x, GRID_W, bsz, seq, s_f, s_b, 4)
        x2 = _out_project(o_x, yc_x, x2, mod3, x_row, row2(g_mix_post[i]), wo, tm=tm_x)
        if not last:
            xc2 = _out_project(o_c_, yc_c, xc2, mod3, c_row, row2(g_mix_post[i]), wo, tm=tm_c)

        j = i // 2
        gpre, gpost = row2(g_ffn_pre[i]), row2(g_ffn_post[i])
        if i % 2 == 0:
            ffn = functools.partial(_dense_mix, gpre=gpre, gpost=gpost, w1=w1[j].astype(BF16),
                                    w3=w3[j].astype(BF16), w2=w2[j].astype(BF16), tf=1408)
        else:
            w_r = jnp.zeros((d, LANES), F32).at[:, :N_EXPERTS].set(w_router[j])
            ffn = functools.partial(_moe_mix, gpre=gpre, gpost=gpost, w_r=w_r, w1=e_w1[j].astype(BF16),
                                    w3=e_w3[j].astype(BF16), w2=e_w2[j].astype(BF16), tmg=512, tf=896)
        x2 = ffn(x2, mod3, x_row, tm=tm_x)
        if not last:
            xc2 = ffn(xc2, mod3, c_row, tm=tm_c)
    return x2.reshape(bsz, seq, d)
```

```python
import functools

import numpy as np
import jax
import jax.numpy as jnp
from jax import lax
from jax.experimental import pallas as pl
from jax.experimental.pallas import tpu as pltpu

F32 = jnp.float32
BF16 = jnp.bfloat16
HIGHEST = lax.Precision.HIGHEST

D_MODEL = 1024
GLA_HEADS = 4
GLA_DK = 64
GLA_DV = 128
GLA_KW = GLA_HEADS * GLA_DK
GLA_VW = GLA_HEADS * GLA_DV
DECAY_RANK = 16
GATE_NORMALIZER = 16.0
CHUNK = 64
CONV_CH = D_MODEL - GLA_VW
GRID_W = 64
N_EXPERTS = 8
EPS = 1e-6
LANES = 128
N_LEVELS = 6
MM_LEVELS = N_LEVELS - 1
MOD_ROWS = 24
VMEM_LIMIT = 52 * 1024 * 1024


def _cparams(sem):
    return pltpu.CompilerParams(dimension_semantics=sem, vmem_limit_bytes=VMEM_LIMIT)


def _rms(x, gain):
    return x * lax.rsqrt(jnp.mean(x * x, axis=-1, keepdims=True) + EPS) * gain


def _silu(x):
    return x / (1.0 + jnp.exp(-x))


def _dot_split(a, b):
    ah = a.astype(BF16)
    al = (a - ah.astype(F32)).astype(BF16)
    bh = b.astype(BF16)
    bl = (b - bh.astype(F32)).astype(BF16)
    dot = functools.partial(jnp.dot, preferred_element_type=F32)
    return dot(ah, bh) + (dot(ah, bl) + dot(al, bh))


def _mod_kernel(c_ref, w_ref, b_ref, o_ref):
    s = _silu(c_ref[...])
    o_ref[...] = jnp.dot(s, w_ref[...], precision=HIGHEST, preferred_element_type=F32) + b_ref[...]


def _modulation(c_all, w_mod, b_mod):
    depth, d, n = w_mod.shape
    tn = 1536
    return pl.pallas_call(
        _mod_kernel,
        grid=(depth, n // tn),
        in_specs=[
            pl.BlockSpec((MOD_ROWS, d), lambda i, j: (0, 0)),
            pl.BlockSpec((None, d, tn), lambda i, j: (i, 0, j)),
            pl.BlockSpec((None, 1, tn), lambda i, j: (i, 0, j)),
        ],
        out_specs=pl.BlockSpec((None, MOD_ROWS, tn), lambda i, j: (i, 0, j)),
        out_shape=jax.ShapeDtypeStruct((depth, MOD_ROWS, n), F32),
        compiler_params=_cparams(("parallel", "parallel")),
        name="modulation",
    )(c_all, w_mod, b_mod.reshape(depth, 1, n))


def _proj_kernel(x_ref, sh_ref, sc_ref, g_ref, wm_ref, wa_ref, wc_ref, wd_ref, bd_ref, cw_ref,
                 q_ref, k_ref, v_ref, gate_ref, gf_ref, gb_ref, yc_ref, *, seg, sub):
    for r0 in range(0, x_ref.shape[0], sub):
        rows = slice(r0, r0 + sub)
        h = _rms(x_ref[rows, :], g_ref[...]) * (1.0 + sc_ref[...]) + sh_ref[...]
        hb = h.astype(BF16)
        p = jnp.dot(hb, wm_ref[...], preferred_element_type=F32)
        q_ref[rows, :] = (p[:, :GLA_KW] * (GLA_DK ** -0.5)).astype(BF16)
        k_ref[rows, :] = p[:, GLA_KW:2 * GLA_KW].astype(BF16)
        v_ref[rows, :] = p[:, 2 * GLA_KW:2 * GLA_KW + GLA_VW].astype(BF16)
        gate_ref[rows, :] = p[:, 2 * GLA_KW + GLA_VW:].astype(BF16)
        pa = jnp.dot(hb, wa_ref[...], preferred_element_type=F32)
        xd = _dot_split(pa, wd_ref[...]) + bd_ref[...]
        ls = (jnp.minimum(xd, 0.0) - jnp.log1p(jnp.exp(-jnp.abs(xd)))) * (1.0 / GATE_NORMALIZER)
        gf_ref[rows, :] = ls[:, :GLA_KW]
        gb_ref[rows, :] = ls[:, GLA_KW:]
        pc = jnp.dot(hb, wc_ref[...], preferred_element_type=F32)
        u = pc[:, CONV_CH:2 * CONV_CH] * pc[:, 2 * CONV_CH:]
        row = lax.broadcasted_iota(jnp.int32, u.shape, 0) & (seg - 1)
        u_prev = jnp.where(row == 0, 0.0, pltpu.roll(u, 1, 0))
        u_next = jnp.where(row == seg - 1, 0.0, pltpu.roll(u, sub - 1, 0))
        cw = cw_ref[...]
        yc = pc[:, :CONV_CH] * (cw[0:1] * u_prev + cw[1:2] * u + cw[2:3] * u_next)
        yc_ref[rows, :] = yc.astype(BF16)


def _project(x2, mod3, mod_row, gain, wm, wa, wc, wd, bd, cw, *, tm, seg):
    n = x2.shape[0]
    d = D_MODEL
    row = lambda c: pl.BlockSpec((None, 1, d), lambda i: (mod_row(i), 0, c))
    const = lambda a: pl.BlockSpec(a.shape, lambda i: (0,) * a.ndim, pipeline_mode=pl.Buffered(1))
    tile = lambda w: pl.BlockSpec((tm, w), lambda i: (i, 0))
    shapes = [(GLA_KW, BF16), (GLA_KW, BF16), (GLA_VW, BF16), (GLA_VW, BF16),
              (GLA_KW, F32), (GLA_KW, F32), (CONV_CH, BF16)]
    return pl.pallas_call(
        functools.partial(_proj_kernel, seg=seg, sub=min(tm, 512)),
        grid=(n // tm,),
        in_specs=[tile(d), row(0), row(1), const(gain), const(wm), const(wa), const(wc),
                  const(wd), const(bd), const(cw)],
        out_specs=[tile(w) for w, _ in shapes],
        out_shape=[jax.ShapeDtypeStruct((n, w), t) for w, t in shapes],
        compiler_params=_cparams(("parallel",)),
        name="in_proj",
    )(x2, mod3, mod3, gain, wm, wa, wc, wd, bd, cw)


def _gla_constants():
    c = CHUNK
    main = np.zeros((2 * MM_LEVELS + 2, c, 2 * c), np.float32)
    lmask = np.zeros((N_LEVELS + 1, c, c), np.float32)
    for lvl in range(N_LEVELS):
        s = c >> (lvl + 1)
        for i in range(c):
            mid = (i // (2 * s)) * 2 * s + s
            if lvl < MM_LEVELS and i >= mid:
                main[2 * lvl, i, mid:i + 1] = 1.0
                main[2 * lvl + 1, i, c + mid:c + i] = 1.0
            elif lvl < MM_LEVELS:
                main[2 * lvl, i, c + i:c + mid] = 1.0
                main[2 * lvl + 1, i, i + 1:mid] = 1.0
            for j in range(c):
                same = (j // (2 * s)) == (i // (2 * s))
                lmask[lvl, i, j] = float(same and ((i >= mid) != (j >= mid)))
    lmask[N_LEVELS] = 2.0 * np.eye(c)
    bwd = np.zeros((2, c, c), np.float32)
    for i in range(c):
        main[2 * MM_LEVELS, i, :i + 1] = 1.0
        main[2 * MM_LEVELS + 1, i, i + 1:c] = 1.0
        bwd[0, i, i:] = 1.0
        bwd[1, i, :i] = 1.0
    main = main.reshape(-1, 2 * c)
    bwd = bwd.reshape(-1, c)
    lmask = np.concatenate([lmask, lmask], axis=-1)
    return jnp.asarray(main, BF16), jnp.asarray(bwd, BF16), jnp.asarray(lmask, F32)


def _sum_dot(n_ref, g, parts):
    gb = g.astype(BF16)
    rows = n_ref.shape[0] // parts
    outs = [jnp.dot(n_ref[p * rows:(p + 1) * rows, :], gb, preferred_element_type=F32) for p in range(parts)]
    per = rows // CHUNK
    return lambda b: outs[b // per][(b % per) * CHUNK:(b % per + 1) * CHUNK]


_NT = (((1,), (1,)), ((), ()))
_TN = (((0,), (0,)), ((), ()))


def _gla_kernel(qf_ref, kf_ref, vf_ref, gff_ref, gbf_ref, qb_ref, kb_ref, vb_ref, gbb_ref,
                gate_ref, gain_ref, s0f_ref, s0b_ref, nmain_ref, nbwd_ref, lmask_ref,
                out_ref, sf_out_ref, sb_out_ref, acc_ref, sf_ref, sb_ref, *, cps):
    n = pl.program_id(1)
    nb = pl.num_programs(1)
    c = CHUNK
    pw = 2 * GLA_DK
    vw = 2 * GLA_DV

    @pl.when(n == 0)
    def _():
        acc_ref[...] = jnp.zeros_like(acc_ref)
        sf_ref[...] = s0f_ref[...]
        sb_ref[...] = s0b_ref[...]

    klane = lax.broadcasted_iota(jnp.int32, (c, GLA_KW), 1)
    k_even = (klane & (pw - 1)) < GLA_DK
    vlane = lax.broadcasted_iota(jnp.int32, (c, vw), 1)
    v_low = vlane < GLA_DV
    srow = lax.broadcasted_iota(jnp.int32, (vw, pw), 0)
    scol = lax.broadcasted_iota(jnp.int32, (vw, pw), 1)
    s_diag = (srow < GLA_DV) == (scol < GLA_DK)
    zero_b = jnp.zeros((), BF16)

    def inter(qi, ki, dec, v, s_ref):
        outs = []
        for p in range(2):
            st = s_ref[p]
            outs.append(lax.dot_general(qi[:, p * pw:(p + 1) * pw], st.astype(BF16), _NT,
                                        preferred_element_type=F32))
            upd = lax.dot_general(v[:, p * vw:(p + 1) * vw], ki[:, p * pw:(p + 1) * pw], _TN,
                                  preferred_element_type=F32)
            s_ref[p] = st * dec[:, p * pw:(p + 1) * pw] + jnp.where(s_diag, upd, 0.0)
        return jnp.concatenate(outs, axis=1)

    odd_row = (lax.broadcasted_iota(jnp.int32, (c, GLA_KW), 0) & 1) == 1

    def fwd_chunk(j):
        r0 = pl.multiple_of(j * c, c)
        rows = pl.ds(r0, c)
        qb16 = qf_ref[rows, :]
        kb16 = kf_ref[rows, :]
        v = vf_ref[rows, :]
        q = qb16.astype(F32)
        k = kb16.astype(F32)
        gf = gff_ref[rows, :]
        gb = gbf_ref[rows, :]
        e = _sum_dot(nmain_ref, jnp.concatenate([gf, gb], axis=0), 2)
        att = [jnp.zeros((c, pw), F32), jnp.zeros((c, pw), F32)]
        for lvl in range(N_LEVELS + 1):
            if lvl < MM_LEVELS:
                qt = (q * jnp.exp(e(2 * lvl))).astype(BF16)
                kt = (k * jnp.exp(e(2 * lvl + 1))).astype(BF16)
            elif lvl < N_LEVELS:
                qt, kt = (q * jnp.exp(jnp.where(odd_row, gf, gb))).astype(BF16), kb16
            else:
                qt, kt = qb16, kb16
            ke = jnp.where(k_even, kt, zero_b)
            ko = jnp.where(k_even, zero_b, kt)
            m = lmask_ref[lvl]
            for p in range(2):
                kbd = jnp.concatenate([ke[:, p * pw:(p + 1) * pw], ko[:, p * pw:(p + 1) * pw]], axis=0)
                att[p] = att[p] + m * lax.dot_general(qt[:, p * pw:(p + 1) * pw], kbd, _NT,
                                                      preferred_element_type=F32)
        outs = []
        for p in range(2):
            vp = v[:, p * vw:(p + 1) * vw]
            vbd = jnp.concatenate([jnp.where(v_low, vp, zero_b), jnp.where(v_low, zero_b, vp)], axis=0)
            outs.append(jnp.dot(att[p].astype(BF16), vbd, preferred_element_type=F32))
        o = jnp.concatenate(outs, axis=1)
        cum = e(2 * MM_LEVELS)
        qi = (q * jnp.exp(cum)).astype(BF16)
        ki = (k * jnp.exp(e(2 * MM_LEVELS + 1))).astype(BF16)
        o = o + inter(qi, ki, jnp.exp(cum[c - 1:c]), v, sf_ref)
        a0 = pl.multiple_of(n * (cps * c) + r0, c)
        acc_ref[pl.ds(a0, c), :] += o

    def bwd_chunk(j):
        r0 = pl.multiple_of((cps - 1 - j) * c, c)
        rows = pl.ds(r0, c)
        q = qb_ref[rows, :].astype(F32)
        k = kb_ref[rows, :].astype(F32)
        v = vb_ref[rows, :]
        e = _sum_dot(nbwd_ref, gbb_ref[rows, :], 1)
        cum = e(0)
        qi = (q * jnp.exp(cum)).astype(BF16)
        ki = (k * jnp.exp(e(1))).astype(BF16)
        o = inter(qi, ki, jnp.exp(cum[0:1]), v, sb_ref)
        a0 = pl.multiple_of((nb - 1 - n) * (cps * c) + r0, c)
        acc_ref[pl.ds(a0, c), :] += o

    def both(j, carry):
        fwd_chunk(j)
        bwd_chunk(j)
        return carry

    lax.fori_loop(0, cps, both, 0, unroll=True)

    @pl.when(n == nb - 1)
    def _():
        sf_out_ref[...] = sf_ref[...]
        sb_out_ref[...] = sb_ref[...]
        gain = gain_ref[...]
        rb = 256

        def gate_rows(i, carry):
            rows = pl.ds(pl.multiple_of(i * rb, rb), rb)
            o = acc_ref[rows, :]
            gt = gate_ref[rows, :].astype(F32)
            normed = jnp.concatenate(
                [_rms(o[:, h * GLA_DV:(h + 1) * GLA_DV], gain) for h in range(GLA_HEADS)], axis=1)
            out_ref[rows, :] = (normed * _silu(gt)).astype(BF16)
            return carry

        lax.fori_loop(0, acc_ref.shape[0] // rb, gate_rows, 0)


def _gla(q, k, v, gf, gb, gate, gain, s0f, s0b, consts, *, cps):
    b, l, _ = q.shape
    t = cps * CHUNK
    nb = l // t
    nmain, nbwd, lmask = consts
    fwd = lambda w: pl.BlockSpec((None, t, w), lambda i, n: (i, n, 0))
    bwd = lambda w: pl.BlockSpec((None, t, w), lambda i, n: (i, nb - 1 - n, 0))
    whole = lambda w: pl.BlockSpec((None, l, w), lambda i, n: (i, 0, 0))
    state = pl.BlockSpec((None, 2, 2 * GLA_DV, 2 * GLA_DK), lambda i, n: (i, 0, 0, 0))
    const = lambda a: pl.BlockSpec(a.shape, lambda i, n: (0,) * a.ndim)
    sshape = jax.ShapeDtypeStruct((b, 2, 2 * GLA_DV, 2 * GLA_DK), F32)
    return pl.pallas_call(
        functools.partial(_gla_kernel, cps=cps),
        grid=(b, nb),
        in_specs=[fwd(GLA_KW), fwd(GLA_KW), fwd(GLA_VW), fwd(GLA_KW), fwd(GLA_KW),
                  bwd(GLA_KW), bwd(GLA_KW), bwd(GLA_VW), bwd(GLA_KW),
                  whole(GLA_VW), const(gain), state, state, const(nmain), const(nbwd), const(lmask)],
        out_specs=[whole(GLA_VW), state, state],
        out_shape=[jax.ShapeDtypeStruct((b, l, GLA_VW), BF16), sshape, sshape],
        scratch_shapes=[pltpu.VMEM((l, GLA_VW), F32),
                        pltpu.VMEM((2, 2 * GLA_DV, 2 * GLA_DK), F32),
                        pltpu.VMEM((2, 2 * GLA_DV, 2 * GLA_DK), F32)],
        compiler_params=_cparams(("parallel", "arbitrary")),
        name="gla_scan",
    )(q, k, v, gf, gb, q, k, v, gb, gate, gain, s0f, s0b, nmain, nbwd, lmask)


MXU_TILE = 256


def _ff_chunks(ff, tf):
    return [(f0, min(f0 + tf, ff)) for f0 in range(0, ff, tf)]


def _swiglu(xs, w1_ref, w3_ref, w2_ref, tf):
    k = xs[0].shape[1]
    acc = None
    for f0, f1 in _ff_chunks(w1_ref.shape[1], tf):
        a = sum(jnp.dot(x, w1_ref[r * k:(r + 1) * k, f0:f1], preferred_element_type=F32) for r, x in enumerate(xs))
        b = sum(jnp.dot(x, w3_ref[r * k:(r + 1) * k, f0:f1], preferred_element_type=F32) for r, x in enumerate(xs))
        y = jnp.dot((_silu(a) * b).astype(BF16), w2_ref[f0:f1, :], preferred_element_type=F32)
        acc = y if acc is None else acc + y
    return acc


def _token_mix_residual(x_ref, gla_ref, yc_ref, gt_ref, gp_ref, wo_ref, rows=slice(None)):
    y = (jnp.dot(gla_ref[rows, :], wo_ref[:GLA_VW, :], preferred_element_type=F32)
         + jnp.dot(yc_ref[rows, :], wo_ref[GLA_VW:, :], preferred_element_type=F32))
    return x_ref[rows, :] + gt_ref[...] * _rms(y, gp_ref[...])


def _mix_specs(mod_row, gpm, wo, tm):
    d = D_MODEL
    return [pl.BlockSpec((tm, d), lambda i: (i, 0)), pl.BlockSpec((tm, GLA_VW), lambda i: (i, 0)),
            pl.BlockSpec((tm, CONV_CH), lambda i: (i, 0)),
            pl.BlockSpec((None, 1, d), lambda i: (mod_row(i), 0, 2)),
            pl.BlockSpec(gpm.shape, lambda i: (0, 0)),
            pl.BlockSpec(wo.shape, lambda i: (0, 0), pipeline_mode=pl.Buffered(1))]


def _dense_kernel(x_ref, gla_ref, yc_ref, gt1_ref, gpm_ref, wo_ref, sh_ref, sc_ref, gt_ref, gpre_ref, gpost_ref,
                  w1_ref, w3_ref, w2_ref, o_ref, *, tf, sub):
    for r0 in range(0, x_ref.shape[0], sub):
        rows = slice(r0, r0 + sub)
        x1 = _token_mix_residual(x_ref, gla_ref, yc_ref, gt1_ref, gpm_ref, wo_ref, rows)
        h = _rms(x1, gpre_ref[...]) * (1.0 + sc_ref[...]) + sh_ref[...]
        y = _swiglu([h.astype(BF16)], w1_ref, w3_ref, w2_ref, tf)
        o_ref[rows, :] = x1 + gt_ref[...] * _rms(y, gpost_ref[...])


def _dense_mix(x2, gla, yc, mod3, mod_row, gpm, wo, gpre, gpost, w1, w3, w2, *, tm, tf):
    n, d = x2.shape
    row = lambda c: pl.BlockSpec((None, 1, d), lambda i: (mod_row(i), 0, c))
    const = lambda a: pl.BlockSpec(a.shape, lambda i: (0,) * a.ndim)
    held = lambda a: pl.BlockSpec(a.shape, lambda i: (0,) * a.ndim, pipeline_mode=pl.Buffered(1))
    return pl.pallas_call(
        functools.partial(_dense_kernel, tf=tf, sub=min(tm, 512)),
        grid=(n // tm,),
        in_specs=_mix_specs(mod_row, gpm, wo, tm) + [row(3), row(4), row(5), const(gpre), const(gpost),
                                                     held(w1), held(w3), held(w2)],
        out_specs=pl.BlockSpec((tm, d), lambda i: (i, 0)),
        out_shape=jax.ShapeDtypeStruct((n, d), F32),
        compiler_params=_cparams(("parallel",)),
        name="dense_mix",
    )(x2, gla, yc, mod3, gpm, wo, mod3, mod3, mod3, gpre, gpost, w1, w3, w2)


HALF = D_MODEL // 2
HI_MASK = 0xFFFF0000


def _pack_rows(v):
    bits = pltpu.bitcast(v.astype(BF16).astype(F32), jnp.uint32)
    return (bits[:, :HALF] >> 16) | (bits[:, HALF:] & jnp.uint32(HI_MASK))


def _unpack_rows(w):
    lo = pltpu.bitcast(w << 16, F32)
    hi = pltpu.bitcast(w & jnp.uint32(HI_MASK), F32)
    return jnp.concatenate([lo, hi], axis=1)


def _route_kernel(x_ref, gla_ref, yc_ref, gt1_ref, gpm_ref, wo_ref, sh_ref, sc_ref, gpre_ref, wr_ref, tri_ref,
                  x1_ref, hp_ref, info_ref, cnt_ref, carry_ref):
    i = pl.program_id(0)

    @pl.when(i == 0)
    def _():
        carry_ref[...] = jnp.zeros_like(carry_ref)

    x1 = _token_mix_residual(x_ref, gla_ref, yc_ref, gt1_ref, gpm_ref, wo_ref)
    x1_ref[...] = x1
    h = _rms(x1, gpre_ref[...]) * (1.0 + sc_ref[...]) + sh_ref[...]
    hp_ref[...] = _pack_rows(h)
    logits = _dot_split(h, wr_ref[...])
    lane = lax.broadcasted_iota(jnp.int32, logits.shape, 1).astype(F32)
    logits = jnp.where(lane < N_EXPERTS, logits, -jnp.inf)
    m1 = jnp.max(logits, axis=-1, keepdims=True)
    i1 = jnp.min(jnp.where(logits == m1, lane, float(LANES)), axis=-1, keepdims=True)
    rest = jnp.where(lane == i1, -jnp.inf, logits)
    m2 = jnp.max(rest, axis=-1, keepdims=True)
    i2 = jnp.min(jnp.where(rest == m2, lane, float(LANES)), axis=-1, keepdims=True)
    e2 = jnp.exp(m2 - m1)
    den = 1.0 + e2
    pick = jnp.where((lane == i1) | (lane == i2), 1.0, 0.0)
    rank = jnp.dot(tri_ref[...], pick.astype(BF16), preferred_element_type=F32) + carry_ref[...]
    carry_ref[...] += jnp.sum(pick, axis=0, keepdims=True)
    cnt_ref[...] = carry_ref[...]
    r1 = jnp.sum(jnp.where(lane == i1, rank, 0.0), axis=-1, keepdims=True)
    r2 = jnp.sum(jnp.where(lane == i2, rank, 0.0), axis=-1, keepdims=True)
    info = jnp.zeros_like(logits)
    for col, val in enumerate((i1, i2, 1.0 / den, e2 / den, r1, r2)):
        info = jnp.where(lane == col, val, info)
    info_ref[...] = info


def _route(x2, gla, yc, mod3, mod_row, gpm, wo, gpre, w_r, *, tm):
    n, d = x2.shape
    row = lambda c: pl.BlockSpec((None, 1, d), lambda i: (mod_row(i), 0, c))
    const = lambda a: pl.BlockSpec(a.shape, lambda i: (0,) * a.ndim)
    tri = jnp.asarray(np.tril(np.ones((tm, tm), np.float32), -1), BF16)
    return pl.pallas_call(
        _route_kernel,
        grid=(n // tm,),
        in_specs=_mix_specs(mod_row, gpm, wo, tm) + [row(3), row(4), const(gpre), const(w_r), const(tri)],
        out_specs=[pl.BlockSpec((tm, d), lambda i: (i, 0)), pl.BlockSpec((tm, HALF), lambda i: (i, 0)),
                   pl.BlockSpec((tm, LANES), lambda i: (i, 0)), pl.BlockSpec((1, LANES), lambda i: (0, 0))],
        out_shape=[jax.ShapeDtypeStruct((n, d), F32), jax.ShapeDtypeStruct((n, HALF), jnp.uint32),
                   jax.ShapeDtypeStruct((n, LANES), F32), jax.ShapeDtypeStruct((1, LANES), F32)],
        scratch_shapes=[pltpu.VMEM((1, LANES), F32)],
        compiler_params=_cparams(("arbitrary",)),
        name="moe_route",
    )(x2, gla, yc, mod3, gpm, wo, mod3, mod3, gpre, w_r, tri)


def _row_copy(src, s, dst, t, sem):
    return pltpu.make_async_copy(src.at[pl.ds(s, 1), :], dst.at[pl.ds(t, 1), :], sem)


DMA_UNROLL = 8


def _dispatch_kernel(pos_ref, hp_ref, xs_in_ref, xs_ref, sem, *, tm):
    del xs_in_ref

    def start(t, carry):
        _row_copy(hp_ref, t, xs_ref, pos_ref[0, 2 * t], sem).start()
        _row_copy(hp_ref, t, xs_ref, pos_ref[0, 2 * t + 1], sem).start()
        return carry

    def wait(t, carry):
        _row_copy(hp_ref, 0, xs_ref, 0, sem).wait()
        return carry

    lax.fori_loop(0, tm, start, 0, unroll=DMA_UNROLL)
    lax.fori_loop(0, 2 * tm, wait, 0, unroll=DMA_UNROLL)


def _dispatch(pos3, hp, n_sorted, *, tm):
    n = hp.shape[0]
    xs0 = jnp.zeros((n_sorted, HALF), jnp.uint32)
    return pl.pallas_call(
        functools.partial(_dispatch_kernel, tm=tm),
        grid=(n // tm,),
        in_specs=[pl.BlockSpec((None, 1, 2 * tm), lambda i: (i, 0, 0), memory_space=pltpu.SMEM),
                  pl.BlockSpec((tm, HALF), lambda i: (i, 0)), pl.BlockSpec(memory_space=pl.ANY)],
        out_specs=pl.BlockSpec(memory_space=pl.ANY),
        out_shape=jax.ShapeDtypeStruct((n_sorted, HALF), jnp.uint32),
        scratch_shapes=[pltpu.SemaphoreType.DMA],
        input_output_aliases={2: 0},
        compiler_params=pltpu.CompilerParams(dimension_semantics=("arbitrary",), has_side_effects=True),
        name="moe_dispatch",
    )(pos3, hp, xs0)


def _expert_kernel(te_ref, nu_ref, xs_ref, w1_ref, w3_ref, w2_ref, ys_ref, *, tf):
    del te_ref

    @pl.when(pl.program_id(0) < nu_ref[0])
    def _():
        xw = xs_ref[...]
        lo = pltpu.bitcast(xw << 16, F32).astype(BF16)
        hi = pltpu.bitcast(xw & jnp.uint32(HI_MASK), F32).astype(BF16)
        ys_ref[...] = _pack_rows(_swiglu([lo, hi], w1_ref, w3_ref, w2_ref, tf))


def _experts(tile_expert, n_used, xs, w1, w3, w2, *, tmg, tf):
    n_sorted = xs.shape[0]
    _, d, ff = w1.shape
    blk = lambda j, te, nu: (jnp.minimum(j, nu[0] - 1), 0)
    grid_spec = pltpu.PrefetchScalarGridSpec(
        num_scalar_prefetch=2,
        grid=(n_sorted // tmg,),
        in_specs=[pl.BlockSpec((tmg, HALF), blk),
                  pl.BlockSpec((None, d, ff), lambda j, te, nu: (te[j], 0, 0), pipeline_mode=pl.Buffered(1)),
                  pl.BlockSpec((None, d, ff), lambda j, te, nu: (te[j], 0, 0), pipeline_mode=pl.Buffered(1)),
                  pl.BlockSpec((None, ff, d), lambda j, te, nu: (te[j], 0, 0), pipeline_mode=pl.Buffered(1))],
        out_specs=pl.BlockSpec((tmg, HALF), blk),
    )
    return pl.pallas_call(
        functools.partial(_expert_kernel, tf=tf),
        grid_spec=grid_spec,
        out_shape=jax.ShapeDtypeStruct((n_sorted, HALF), jnp.uint32),
        compiler_params=_cparams(("arbitrary",)),
        name="moe_experts",
    )(tile_expert, n_used, xs, w1, w3, w2)


def _combine_kernel(pos_ref, x_ref, gt_ref, gpost_ref, info_ref, ys_ref, o_ref, buf_ref, sems, *, tm):
    i = pl.program_id(0)
    slot = i & 1

    @pl.when(i < pl.num_programs(0) - 1)
    def _():
        def start(t, carry):
            _row_copy(ys_ref, pos_ref[0, 2 * t], buf_ref.at[slot, 0], t, sems.at[slot]).start()
            _row_copy(ys_ref, pos_ref[0, 2 * t + 1], buf_ref.at[slot, 1], t, sems.at[slot]).start()
            return carry

        lax.fori_loop(0, tm, start, 0, unroll=DMA_UNROLL)

    @pl.when(i > 0)
    def _():
        prev = 1 - slot

        def wait(t, carry):
            _row_copy(ys_ref, 0, buf_ref.at[prev, 0], 0, sems.at[prev]).wait()
            return carry

        lax.fori_loop(0, 2 * tm, wait, 0, unroll=DMA_UNROLL)
        info = info_ref[...]
        lane = lax.broadcasted_iota(jnp.int32, info.shape, 1)
        wt1 = jnp.sum(jnp.where(lane == 2, info, 0.0), axis=-1, keepdims=True)
        wt2 = jnp.sum(jnp.where(lane == 3, info, 0.0), axis=-1, keepdims=True)
        y = wt1 * _unpack_rows(buf_ref[prev, 0]) + wt2 * _unpack_rows(buf_ref[prev, 1])
        o_ref[...] = x_ref[...] + gt_ref[...] * _rms(y, gpost_ref[...])


def _combine(pos3, x2, mod3, mod_row, gpost, info, ys, *, tm):
    n, d = x2.shape
    nt = n // tm
    done = lambda i: jnp.maximum(i - 1, 0)
    tile = pl.BlockSpec((tm, d), lambda i: (done(i), 0))
    return pl.pallas_call(
        functools.partial(_combine_kernel, tm=tm),
        grid=(nt + 1,),
        in_specs=[pl.BlockSpec((None, 1, 2 * tm), lambda i: (jnp.minimum(i, nt - 1), 0, 0),
                               memory_space=pltpu.SMEM),
                  tile, pl.BlockSpec((None, 1, d), lambda i: (mod_row(done(i)), 0, 5)),
                  pl.BlockSpec(gpost.shape, lambda i: (0, 0)),
                  pl.BlockSpec((tm, LANES), lambda i: (done(i), 0)), pl.BlockSpec(memory_space=pl.ANY)],
        out_specs=tile,
        out_shape=jax.ShapeDtypeStruct((n, d), F32),
        scratch_shapes=[pltpu.VMEM((2, 2, tm, HALF), jnp.uint32), pltpu.SemaphoreType.DMA((2,))],
        compiler_params=_cparams(("arbitrary",)),
        name="moe_combine",
    )(pos3, x2, mod3, gpost, info, ys)


def _moe_mix(x2, gla, yc, mod3, mod_row, gpm, wo, gpre, gpost, w_r, w1, w3, w2, *, tm, tmg, tf):
    n = x2.shape[0]
    x2, hp, info, cnt = _route(x2, gla, yc, mod3, mod_row, gpm, wo, gpre, w_r, tm=tm)
    counts = cnt[0, :N_EXPERTS].astype(jnp.int32)
    padded = ((counts + tmg - 1) // tmg) * tmg
    ends = jnp.cumsum(padded)
    starts = ends - padded
    n_tiles = (2 * n) // tmg + N_EXPERTS
    picks = info[:, 0:2].astype(jnp.int32)
    pos = starts[picks] + info[:, 4:6].astype(jnp.int32)
    pos3 = pos.reshape(n // tm, 1, 2 * tm)
    tile_expert = jnp.sum(jnp.arange(n_tiles, dtype=jnp.int32)[:, None] * tmg >= ends[None, :], axis=1)
    n_used = (ends[-1] // tmg).astype(jnp.int32).reshape(1)
    tile_expert = jnp.minimum(tile_expert, tile_expert[jnp.maximum(n_used[0] - 1, 0)]).astype(jnp.int32)
    xs = _dispatch(pos3, hp, n_tiles * tmg, tm=tm)
    ys = _experts(tile_expert, n_used, xs, w1, w3, w2, tmg=tmg, tf=tf)
    return _combine(pos3, x2, mod3, mod_row, gpost, info, ys, tm=tm)


def kernel(x, c, ctx, c_ctx, w_mod, b_mod, g_mix_pre, g_mix_post, w_in, w_decay, b_decay, gla_norm,
           conv_w, w_out, g_ffn_pre, g_ffn_post, w1, w3, w2, w_router, e_w1, e_w3, e_w2):
    bsz, seq, d = x.shape
    ctx_len = ctx.shape[1]
    depth = w_mod.shape[0]
    assert d == D_MODEL and seq % (4 * CHUNK) == 0 and ctx_len % CHUNK == 0 and bsz + 1 <= MOD_ROWS

    c_all = jnp.zeros((MOD_ROWS, d), F32).at[:bsz].set(c).at[bsz].set(c_ctx)
    mod = _modulation(c_all, w_mod, b_mod)
    consts = _gla_constants()

    tm_x = 512
    tm_p = 1024
    tm_c = ctx_len
    assert seq % tm_p == 0 and seq % tm_x == 0 and ctx_len & (ctx_len - 1) == 0
    x_row = lambda i: i // (seq // tm_x)
    c_row = lambda i: bsz
    x2 = x.reshape(bsz * seq, d)
    xc2 = ctx.reshape(bsz * ctx_len, d)
    zero_state = jnp.zeros((bsz, 2, 2 * GLA_DV, 2 * GLA_DK), F32)
    row2 = lambda a: a.reshape(1, -1)

    o_q, o_k, o_v = 0, GLA_KW, 2 * GLA_KW
    o_g = o_v + GLA_VW
    o_a = o_g + GLA_VW
    o_c = o_a + 2 * DECAY_RANK

    for i in range(depth):
        last = i == depth - 1
        mod3 = mod[i].reshape(MOD_ROWS, 1, 6 * d)
        wi = w_in[i]
        wm = wi[:, :o_a].astype(BF16)
        wa = jnp.zeros((d, LANES), F32).at[:, :2 * DECAY_RANK].set(wi[:, o_a:o_c]).astype(BF16)
        wc = wi[:, o_c:].astype(BF16)
        wd = jnp.zeros((LANES, 2 * GLA_KW), F32)
        wd = wd.at[:DECAY_RANK, :GLA_KW].set(w_decay[i, 0]).at[DECAY_RANK:2 * DECAY_RANK, GLA_KW:].set(w_decay[i, 1])
        bd = b_decay[i].reshape(1, 2 * GLA_KW)
        wo = w_out[i].astype(BF16)
        gain = row2(gla_norm[i])

        def mix(tokens, mod_row, tm, seg, nseq, slen, s0f, s0b, cps):
            q, k, v, gate, gf, gb, yc = _project(tokens, mod3, mod_row, row2(g_mix_pre[i]), wm, wa, wc, wd, bd,
                                                 conv_w[i], tm=tm, seg=seg)
            r3 = lambda a: a.reshape(nseq, slen, a.shape[-1])
            o, sf, sb = _gla(r3(q), r3(k), r3(v), r3(gf), r3(gb), r3(gate), gain, s0f, s0b, consts, cps=cps)
            return o.reshape(nseq * slen, GLA_VW), yc, sf, sb

        o_c_, yc_c, s_f, s_b = mix(xc2, c_row, tm_c, ctx_len, bsz, ctx_len, zero_state, zero_state,
                                   ctx_len // CHUNK)
        o_x, yc_x, _, _ = mix(x2, lambda t: t // (seq // tm_p), tm_p, GRID_W, bsz, seq, s_f, s_b, 4)

        j = i // 2
        common = dict(gpm=row2(g_mix_post[i]), wo=wo, gpre=row2(g_ffn_pre[i]), gpost=row2(g_ffn_post[i]))
        if i % 2 == 0:
            ffn = functools.partial(_dense_mix, w1=w1[j].astype(BF16), w3=w3[j].astype(BF16),
                                    w2=w2[j].astype(BF16), tf=4 * MXU_TILE, **common)
        else:
            w_r = jnp.zeros((d, LANES), F32).at[:, :N_EXPERTS].set(w_router[j])
            ffn = functools.partial(_moe_mix, w_r=w_r, w1=e_w1[j].astype(BF16), w3=e_w3[j].astype(BF16),
                                    w2=e_w2[j].astype(BF16), tmg=512, tf=7 * MXU_TILE, **common)
        x2 = ffn(x2, o_x, yc_x, mod3, x_row, tm=tm_x)
        if not last:
            xc2 = ffn(xc2, o_c_, yc_c, mod3, c_row, tm=tm_c)
    return x2.reshape(bsz, seq, d)
```

```python
import functools

import numpy as np
import jax
import jax.numpy as jnp
from jax import lax
from jax.experimental import pallas as pl
from jax.experimental.pallas import tpu as pltpu

F32 = jnp.float32
BF16 = jnp.bfloat16
HIGHEST = lax.Precision.HIGHEST

D_MODEL = 1024
GLA_HEADS = 4
GLA_DK = 64
GLA_DV = 128
GLA_KW = GLA_HEADS * GLA_DK
GLA_VW = GLA_HEADS * GLA_DV
DECAY_RANK = 16
GATE_NORMALIZER = 16.0
CHUNK = 64
CONV_CH = D_MODEL - GLA_VW
GRID_W = 64
N_EXPERTS = 8
EPS = 1e-6
LANES = 128
N_LEVELS = 6
MM_LEVELS = N_LEVELS - 1
MOD_ROWS = 24
VMEM_LIMIT = 52 * 1024 * 1024


def _cparams(sem):
    return pltpu.CompilerParams(dimension_semantics=sem, vmem_limit_bytes=VMEM_LIMIT)


def _rms(x, gain):
    return x * lax.rsqrt(jnp.mean(x * x, axis=-1, keepdims=True) + EPS) * gain


def _silu(x):
    return x / (1.0 + jnp.exp(-x))


def _dot_split(a, b):
    ah = a.astype(BF16)
    al = (a - ah.astype(F32)).astype(BF16)
    bh = b.astype(BF16)
    bl = (b - bh.astype(F32)).astype(BF16)
    dot = functools.partial(jnp.dot, preferred_element_type=F32)
    return dot(ah, bh) + (dot(ah, bl) + dot(al, bh))


def _mod_kernel(c_ref, w_ref, b_ref, o_ref):
    s = _silu(c_ref[...])
    o_ref[...] = jnp.dot(s, w_ref[...], precision=HIGHEST, preferred_element_type=F32) + b_ref[...]


def _modulation(c_all, w_mod, b_mod):
    depth, d, n = w_mod.shape
    tn = 1536
    return pl.pallas_call(
        _mod_kernel,
        grid=(depth, n // tn),
        in_specs=[
            pl.BlockSpec((MOD_ROWS, d), lambda i, j: (0, 0)),
            pl.BlockSpec((None, d, tn), lambda i, j: (i, 0, j)),
            pl.BlockSpec((None, 1, tn), lambda i, j: (i, 0, j)),
        ],
        out_specs=pl.BlockSpec((None, MOD_ROWS, tn), lambda i, j: (i, 0, j)),
        out_shape=jax.ShapeDtypeStruct((depth, MOD_ROWS, n), F32),
        compiler_params=_cparams(("parallel", "parallel")),
        name="modulation",
    )(c_all, w_mod, b_mod.reshape(depth, 1, n))


def _proj_kernel(x_ref, sh_ref, sc_ref, g_ref, wm_ref, wa_ref, wc_ref, wd_ref, bd_ref, cw_ref,
                 q_ref, k_ref, v_ref, gate_ref, gf_ref, gb_ref, yc_ref, *, seg, sub):
    for r0 in range(0, x_ref.shape[0], sub):
        rows = slice(r0, r0 + sub)
        h = _rms(x_ref[rows, :], g_ref[...]) * (1.0 + sc_ref[...]) + sh_ref[...]
        hb = h.astype(BF16)
        p = jnp.dot(hb, wm_ref[...], preferred_element_type=F32)
        q_ref[rows, :] = (p[:, :GLA_KW] * (GLA_DK ** -0.5)).astype(BF16)
        k_ref[rows, :] = p[:, GLA_KW:2 * GLA_KW].astype(BF16)
        v_ref[rows, :] = p[:, 2 * GLA_KW:2 * GLA_KW + GLA_VW].astype(BF16)
        gate_ref[rows, :] = p[:, 2 * GLA_KW + GLA_VW:].astype(BF16)
        pa = jnp.dot(hb, wa_ref[...], preferred_element_type=F32)
        xd = _dot_split(pa, wd_ref[...]) + bd_ref[...]
        ls = (jnp.minimum(xd, 0.0) - jnp.log1p(jnp.exp(-jnp.abs(xd)))) * (1.0 / GATE_NORMALIZER)
        gf_ref[rows, :] = ls[:, :GLA_KW]
        gb_ref[rows, :] = ls[:, GLA_KW:]
        pc = jnp.dot(hb, wc_ref[...], preferred_element_type=F32)
        u = pc[:, CONV_CH:2 * CONV_CH] * pc[:, 2 * CONV_CH:]
        row = lax.broadcasted_iota(jnp.int32, u.shape, 0) & (seg - 1)
        u_prev = jnp.where(row == 0, 0.0, pltpu.roll(u, 1, 0))
        u_next = jnp.where(row == seg - 1, 0.0, pltpu.roll(u, sub - 1, 0))
        cw = cw_ref[...]
        yc = pc[:, :CONV_CH] * (cw[0:1] * u_prev + cw[1:2] * u + cw[2:3] * u_next)
        yc_ref[rows, :] = yc.astype(BF16)


def _project(x2, mod3, mod_row, gain, wm, wa, wc, wd, bd, cw, *, tm, seg):
    n = x2.shape[0]
    d = D_MODEL
    row = lambda c: pl.BlockSpec((None, 1, d), lambda i: (mod_row(i), 0, c))
    const = lambda a: pl.BlockSpec(a.shape, lambda i: (0,) * a.ndim, pipeline_mode=pl.Buffered(1))
    tile = lambda w: pl.BlockSpec((tm, w), lambda i: (i, 0))
    shapes = [(GLA_KW, BF16), (GLA_KW, BF16), (GLA_VW, BF16), (GLA_VW, BF16),
              (GLA_KW, F32), (GLA_KW, F32), (CONV_CH, BF16)]
    return pl.pallas_call(
        functools.partial(_proj_kernel, seg=seg, sub=min(tm, 512)),
        grid=(n // tm,),
        in_specs=[tile(d), row(0), row(1), const(gain), const(wm), const(wa), const(wc),
                  const(wd), const(bd), const(cw)],
        out_specs=[tile(w) for w, _ in shapes],
        out_shape=[jax.ShapeDtypeStruct((n, w), t) for w, t in shapes],
        compiler_params=_cparams(("parallel",)),
        name="in_proj",
    )(x2, mod3, mod3, gain, wm, wa, wc, wd, bd, cw)


def _gla_constants():
    c = CHUNK
    main = np.zeros((2 * MM_LEVELS + 2, c, 2 * c), np.float32)
    lmask = np.zeros((N_LEVELS + 1, c, c), np.float32)
    for lvl in range(N_LEVELS):
        s = c >> (lvl + 1)
        for i in range(c):
            mid = (i // (2 * s)) * 2 * s + s
            if lvl < MM_LEVELS and i >= mid:
                main[2 * lvl, i, mid:i + 1] = 1.0
                main[2 * lvl + 1, i, c + mid:c + i] = 1.0
            elif lvl < MM_LEVELS:
                main[2 * lvl, i, c + i:c + mid] = 1.0
                main[2 * lvl + 1, i, i + 1:mid] = 1.0
            for j in range(c):
                same = (j // (2 * s)) == (i // (2 * s))
                lmask[lvl, i, j] = float(same and ((i >= mid) != (j >= mid)))
    lmask[N_LEVELS] = 2.0 * np.eye(c)
    bwd = np.zeros((2, c, c), np.float32)
    for i in range(c):
        main[2 * MM_LEVELS, i, :i + 1] = 1.0
        main[2 * MM_LEVELS + 1, i, i + 1:c] = 1.0
        bwd[0, i, i:] = 1.0
        bwd[1, i, :i] = 1.0
    main = main.reshape(-1, 2 * c)
    bwd = bwd.reshape(-1, c)
    lmask = np.concatenate([lmask, lmask], axis=-1)
    return jnp.asarray(main, BF16), jnp.asarray(bwd, BF16), jnp.asarray(lmask, F32)


def _sum_dot(n_ref, g, parts):
    gb = g.astype(BF16)
    rows = n_ref.shape[0] // parts
    outs = [jnp.dot(n_ref[p * rows:(p + 1) * rows, :], gb, preferred_element_type=F32) for p in range(parts)]
    per = rows // CHUNK
    return lambda b: outs[b // per][(b % per) * CHUNK:(b % per + 1) * CHUNK]


_NT = (((1,), (1,)), ((), ()))
_TN = (((0,), (0,)), ((), ()))


def _gla_kernel(qf_ref, kf_ref, vf_ref, gff_ref, gbf_ref, qb_ref, kb_ref, vb_ref, gbb_ref,
                gate_ref, gain_ref, s0f_ref, s0b_ref, nmain_ref, nbwd_ref, lmask_ref,
                out_ref, sf_out_ref, sb_out_ref, acc_ref, sf_ref, sb_ref, *, cps):
    n = pl.program_id(1)
    nb = pl.num_programs(1)
    c = CHUNK
    pw = 2 * GLA_DK
    vw = 2 * GLA_DV

    @pl.when(n == 0)
    def _():
        acc_ref[...] = jnp.zeros_like(acc_ref)
        sf_ref[...] = s0f_ref[...]
        sb_ref[...] = s0b_ref[...]

    klane = lax.broadcasted_iota(jnp.int32, (c, GLA_KW), 1)
    k_even = (klane & (pw - 1)) < GLA_DK
    vlane = lax.broadcasted_iota(jnp.int32, (c, vw), 1)
    v_low = vlane < GLA_DV
    srow = lax.broadcasted_iota(jnp.int32, (vw, pw), 0)
    scol = lax.broadcasted_iota(jnp.int32, (vw, pw), 1)
    s_diag = (srow < GLA_DV) == (scol < GLA_DK)
    zero_b = jnp.zeros((), BF16)

    def inter(qi, ki, dec, v, s_ref):
        outs = []
        for p in range(2):
            st = s_ref[p]
            outs.append(lax.dot_general(qi[:, p * pw:(p + 1) * pw], st.astype(BF16), _NT,
                                        preferred_element_type=F32))
            upd = lax.dot_general(v[:, p * vw:(p + 1) * vw], ki[:, p * pw:(p + 1) * pw], _TN,
                                  preferred_element_type=F32)
            s_ref[p] = st * dec[:, p * pw:(p + 1) * pw] + jnp.where(s_diag, upd, 0.0)
        return jnp.concatenate(outs, axis=1)

    odd_row = (lax.broadcasted_iota(jnp.int32, (c, GLA_KW), 0) & 1) == 1

    def fwd_chunk(j):
        r0 = pl.multiple_of(j * c, c)
        rows = pl.ds(r0, c)
        qb16 = qf_ref[rows, :]
        kb16 = kf_ref[rows, :]
        v = vf_ref[rows, :]
        q = qb16.astype(F32)
        k = kb16.astype(F32)
        gf = gff_ref[rows, :]
        gb = gbf_ref[rows, :]
        e = _sum_dot(nmain_ref, jnp.concatenate([gf, gb], axis=0), 2)
        att = [jnp.zeros((c, pw), F32), jnp.zeros((c, pw), F32)]
        for lvl in range(N_LEVELS + 1):
            if lvl < MM_LEVELS:
                qt = (q * jnp.exp(e(2 * lvl))).astype(BF16)
                kt = (k * jnp.exp(e(2 * lvl + 1))).astype(BF16)
            elif lvl < N_LEVELS:
                qt, kt = (q * jnp.exp(jnp.where(odd_row, gf, gb))).astype(BF16), kb16
            else:
                qt, kt = qb16, kb16
            ke = jnp.where(k_even, kt, zero_b)
            ko = jnp.where(k_even, zero_b, kt)
            m = lmask_ref[lvl]
            for p in range(2):
                kbd = jnp.concatenate([ke[:, p * pw:(p + 1) * pw], ko[:, p * pw:(p + 1) * pw]], axis=0)
                att[p] = att[p] + m * lax.dot_general(qt[:, p * pw:(p + 1) * pw], kbd, _NT,
                                                      preferred_element_type=F32)
        outs = []
        for p in range(2):
            vp = v[:, p * vw:(p + 1) * vw]
            vbd = jnp.concatenate([jnp.where(v_low, vp, zero_b), jnp.where(v_low, zero_b, vp)], axis=0)
            outs.append(jnp.dot(att[p].astype(BF16), vbd, preferred_element_type=F32))
        o = jnp.concatenate(outs, axis=1)
        cum = e(2 * MM_LEVELS)
        qi = (q * jnp.exp(cum)).astype(BF16)
        ki = (k * jnp.exp(e(2 * MM_LEVELS + 1))).astype(BF16)
        o = o + inter(qi, ki, jnp.exp(cum[c - 1:c]), v, sf_ref)
        a0 = pl.multiple_of(n * (cps * c) + r0, c)
        acc_ref[pl.ds(a0, c), :] += o

    def bwd_chunk(j):
        r0 = pl.multiple_of((cps - 1 - j) * c, c)
        rows = pl.ds(r0, c)
        q = qb_ref[rows, :].astype(F32)
        k = kb_ref[rows, :].astype(F32)
        v = vb_ref[rows, :]
        e = _sum_dot(nbwd_ref, gbb_ref[rows, :], 1)
        cum = e(0)
        qi = (q * jnp.exp(cum)).astype(BF16)
        ki = (k * jnp.exp(e(1))).astype(BF16)
        o = inter(qi, ki, jnp.exp(cum[0:1]), v, sb_ref)
        a0 = pl.multiple_of((nb - 1 - n) * (cps * c) + r0, c)
        acc_ref[pl.ds(a0, c), :] += o

    def both(j, carry):
        fwd_chunk(j)
        bwd_chunk(j)
        return carry

    lax.fori_loop(0, cps, both, 0, unroll=True)

    @pl.when(n == nb - 1)
    def _():
        sf_out_ref[...] = sf_ref[...]
        sb_out_ref[...] = sb_ref[...]
        gain = gain_ref[...]
        rb = 256

        def gate_rows(i, carry):
            rows = pl.ds(pl.multiple_of(i * rb, rb), rb)
            o = acc_ref[rows, :]
            gt = gate_ref[rows, :].astype(F32)
            normed = jnp.concatenate(
                [_rms(o[:, h * GLA_DV:(h + 1) * GLA_DV], gain) for h in range(GLA_HEADS)], axis=1)
            out_ref[rows, :] = (normed * _silu(gt)).astype(BF16)
            return carry

        lax.fori_loop(0, acc_ref.shape[0] // rb, gate_rows, 0)


def _gla(q, k, v, gf, gb, gate, gain, s0f, s0b, consts, *, cps):
    b, l, _ = q.shape
    t = cps * CHUNK
    nb = l // t
    nmain, nbwd, lmask = consts
    fwd = lambda w: pl.BlockSpec((None, t, w), lambda i, n: (i, n, 0))
    bwd = lambda w: pl.BlockSpec((None, t, w), lambda i, n: (i, nb - 1 - n, 0))
    whole = lambda w: pl.BlockSpec((None, l, w), lambda i, n: (i, 0, 0))
    state = pl.BlockSpec((None, 2, 2 * GLA_DV, 2 * GLA_DK), lambda i, n: (i, 0, 0, 0))
    const = lambda a: pl.BlockSpec(a.shape, lambda i, n: (0,) * a.ndim)
    sshape = jax.ShapeDtypeStruct((b, 2, 2 * GLA_DV, 2 * GLA_DK), F32)
    return pl.pallas_call(
        functools.partial(_gla_kernel, cps=cps),
        grid=(b, nb),
        in_specs=[fwd(GLA_KW), fwd(GLA_KW), fwd(GLA_VW), fwd(GLA_KW), fwd(GLA_KW),
                  bwd(GLA_KW), bwd(GLA_KW), bwd(GLA_VW), bwd(GLA_KW),
                  whole(GLA_VW), const(gain), state, state, const(nmain), const(nbwd), const(lmask)],
        out_specs=[whole(GLA_VW), state, state],
        out_shape=[jax.ShapeDtypeStruct((b, l, GLA_VW), BF16), sshape, sshape],
        scratch_shapes=[pltpu.VMEM((l, GLA_VW), F32),
                        pltpu.VMEM((2, 2 * GLA_DV, 2 * GLA_DK), F32),
                        pltpu.VMEM((2, 2 * GLA_DV, 2 * GLA_DK), F32)],
        compiler_params=_cparams(("parallel", "arbitrary")),
        name="gla_scan",
    )(q, k, v, gf, gb, q, k, v, gb, gate, gain, s0f, s0b, nmain, nbwd, lmask)


MXU_TILE = 256


def _ff_chunks(ff, tf):
    return [(f0, min(f0 + tf, ff)) for f0 in range(0, ff, tf)]


def _swiglu(xs, w1_ref, w3_ref, w2_ref, tf):
    k = xs[0].shape[1]
    acc = None
    for f0, f1 in _ff_chunks(w1_ref.shape[1], tf):
        a = sum(jnp.dot(x, w1_ref[r * k:(r + 1) * k, f0:f1], preferred_element_type=F32) for r, x in enumerate(xs))
        b = sum(jnp.dot(x, w3_ref[r * k:(r + 1) * k, f0:f1], preferred_element_type=F32) for r, x in enumerate(xs))
        y = jnp.dot((_silu(a) * b).astype(BF16), w2_ref[f0:f1, :], preferred_element_type=F32)
        acc = y if acc is None else acc + y
    return acc


F8 = jnp.float8_e4m3fn
F8_PEAK = 256.0
TINY = 1e-30


def _quant_kernel(w_ref, q_ref, inv_ref, *, rb):
    nblk = w_ref.shape[0] // rb

    def peak(i, m):
        blk = jnp.abs(w_ref[pl.ds(pl.multiple_of(i * rb, rb), rb), :])
        return jnp.maximum(m, jnp.max(blk, axis=0, keepdims=True))

    colmax = lax.fori_loop(0, nblk, peak, jnp.zeros((1, w_ref.shape[1]), F32))
    amax = jnp.maximum(jnp.max(colmax, axis=1, keepdims=True), TINY)
    scale = F8_PEAK / amax

    def cast(i, carry):
        rows = pl.ds(pl.multiple_of(i * rb, rb), rb)
        q_ref[rows, :] = (w_ref[rows, :] * scale).astype(F8)
        return carry

    lax.fori_loop(0, nblk, cast, 0)
    inv_ref[...] = jnp.broadcast_to(amax * (1.0 / F8_PEAK), inv_ref.shape)


def _quantize(w):
    ne, r, c = w.shape
    return pl.pallas_call(
        functools.partial(_quant_kernel, rb=128),
        grid=(ne,),
        in_specs=[pl.BlockSpec((None, r, c), lambda e: (e, 0, 0))],
        out_specs=[pl.BlockSpec((None, r, c), lambda e: (e, 0, 0)), pl.BlockSpec((None, 1, LANES), lambda e: (e, 0, 0))],
        out_shape=[jax.ShapeDtypeStruct((ne, r, c), F8), jax.ShapeDtypeStruct((ne, 1, LANES), F32)],
        compiler_params=_cparams(("parallel",)),
        name="quantize_weights",
    )(w)


def _row_scale(parts):
    amax = functools.reduce(jnp.maximum, [jnp.max(jnp.abs(p), axis=1, keepdims=True) for p in parts])
    amax = jnp.maximum(amax, TINY)
    return F8_PEAK / amax, amax * (1.0 / F8_PEAK)


def _swiglu_f8(xs, w1_ref, w3_ref, w2_ref, inv1, inv3, inv2, tf):
    k = xs[0].shape[1]
    sx, ix = _row_scale(xs)
    x8 = [(x * sx).astype(F8) for x in xs]
    acc = None
    for f0, f1 in _ff_chunks(w1_ref.shape[1], tf):
        a = sum(jnp.dot(x, w1_ref[r * k:(r + 1) * k, f0:f1], preferred_element_type=F32) for r, x in enumerate(x8))
        b = sum(jnp.dot(x, w3_ref[r * k:(r + 1) * k, f0:f1], preferred_element_type=F32) for r, x in enumerate(x8))
        m = _silu(a * (ix * inv1)) * b
        sm, im = _row_scale([m])
        y = jnp.dot((m * sm).astype(F8), w2_ref[f0:f1, :], preferred_element_type=F32) * (im * (ix * inv3) * inv2)
        acc = y if acc is None else acc + y
    return acc


def _token_mix_residual(x_ref, gla_ref, yc_ref, gt_ref, gp_ref, wo_ref, rows=slice(None)):
    y = (jnp.dot(gla_ref[rows, :], wo_ref[:GLA_VW, :], preferred_element_type=F32)
         + jnp.dot(yc_ref[rows, :], wo_ref[GLA_VW:, :], preferred_element_type=F32))
    return x_ref[rows, :] + gt_ref[...] * _rms(y, gp_ref[...])


def _mix_specs(mod_row, gpm, wo, tm):
    d = D_MODEL
    return [pl.BlockSpec((tm, d), lambda i: (i, 0)), pl.BlockSpec((tm, GLA_VW), lambda i: (i, 0)),
            pl.BlockSpec((tm, CONV_CH), lambda i: (i, 0)),
            pl.BlockSpec((None, 1, d), lambda i: (mod_row(i), 0, 2)),
            pl.BlockSpec(gpm.shape, lambda i: (0, 0)),
            pl.BlockSpec(wo.shape, lambda i: (0, 0), pipeline_mode=pl.Buffered(1))]


def _dense_kernel(x_ref, gla_ref, yc_ref, gt1_ref, gpm_ref, wo_ref, sh_ref, sc_ref, gt_ref, gpre_ref, gpost_ref,
                  w1_ref, w3_ref, w2_ref, o_ref, *, tf, sub):
    for r0 in range(0, x_ref.shape[0], sub):
        rows = slice(r0, r0 + sub)
        x1 = _token_mix_residual(x_ref, gla_ref, yc_ref, gt1_ref, gpm_ref, wo_ref, rows)
        h = _rms(x1, gpre_ref[...]) * (1.0 + sc_ref[...]) + sh_ref[...]
        y = _swiglu([h.astype(BF16)], w1_ref, w3_ref, w2_ref, tf)
        o_ref[rows, :] = x1 + gt_ref[...] * _rms(y, gpost_ref[...])


def _dense_mix(x2, gla, yc, mod3, mod_row, gpm, wo, gpre, gpost, w1, w3, w2, *, tm, tf):
    n, d = x2.shape
    row = lambda c: pl.BlockSpec((None, 1, d), lambda i: (mod_row(i), 0, c))
    const = lambda a: pl.BlockSpec(a.shape, lambda i: (0,) * a.ndim)
    held = lambda a: pl.BlockSpec(a.shape, lambda i: (0,) * a.ndim, pipeline_mode=pl.Buffered(1))
    return pl.pallas_call(
        functools.partial(_dense_kernel, tf=tf, sub=min(tm, 512)),
        grid=(n // tm,),
        in_specs=_mix_specs(mod_row, gpm, wo, tm) + [row(3), row(4), row(5), const(gpre), const(gpost),
                                                     held(w1), held(w3), held(w2)],
        out_specs=pl.BlockSpec((tm, d), lambda i: (i, 0)),
        out_shape=jax.ShapeDtypeStruct((n, d), F32),
        compiler_params=_cparams(("parallel",)),
        name="dense_mix",
    )(x2, gla, yc, mod3, gpm, wo, mod3, mod3, mod3, gpre, gpost, w1, w3, w2)


HALF = D_MODEL // 2
HI_MASK = 0xFFFF0000


def _pack_rows(v):
    bits = pltpu.bitcast(v.astype(BF16).astype(F32), jnp.uint32)
    return (bits[:, :HALF] >> 16) | (bits[:, HALF:] & jnp.uint32(HI_MASK))


def _unpack_rows(w):
    lo = pltpu.bitcast(w << 16, F32)
    hi = pltpu.bitcast(w & jnp.uint32(HI_MASK), F32)
    return jnp.concatenate([lo, hi], axis=1)


def _route_kernel(x_ref, gla_ref, yc_ref, gt1_ref, gpm_ref, wo_ref, sh_ref, sc_ref, gpre_ref, wr_ref, tri_ref,
                  x1_ref, hp_ref, info_ref, cnt_ref, carry_ref):
    i = pl.program_id(0)

    @pl.when(i == 0)
    def _():
        carry_ref[...] = jnp.zeros_like(carry_ref)

    x1 = _token_mix_residual(x_ref, gla_ref, yc_ref, gt1_ref, gpm_ref, wo_ref)
    x1_ref[...] = x1
    h = _rms(x1, gpre_ref[...]) * (1.0 + sc_ref[...]) + sh_ref[...]
    hp_ref[...] = _pack_rows(h)
    logits = _dot_split(h, wr_ref[...])
    lane = lax.broadcasted_iota(jnp.int32, logits.shape, 1).astype(F32)
    logits = jnp.where(lane < N_EXPERTS, logits, -jnp.inf)
    m1 = jnp.max(logits, axis=-1, keepdims=True)
    i1 = jnp.min(jnp.where(logits == m1, lane, float(LANES)), axis=-1, keepdims=True)
    rest = jnp.where(lane == i1, -jnp.inf, logits)
    m2 = jnp.max(rest, axis=-1, keepdims=True)
    i2 = jnp.min(jnp.where(rest == m2, lane, float(LANES)), axis=-1, keepdims=True)
    e2 = jnp.exp(m2 - m1)
    den = 1.0 + e2
    pick = jnp.where((lane == i1) | (lane == i2), 1.0, 0.0)
    rank = jnp.dot(tri_ref[...], pick.astype(BF16), preferred_element_type=F32) + carry_ref[...]
    carry_ref[...] += jnp.sum(pick, axis=0, keepdims=True)
    cnt_ref[...] = carry_ref[...]
    r1 = jnp.sum(jnp.where(lane == i1, rank, 0.0), axis=-1, keepdims=True)
    r2 = jnp.sum(jnp.where(lane == i2, rank, 0.0), axis=-1, keepdims=True)
    info = jnp.zeros_like(logits)
    for col, val in enumerate((i1, i2, 1.0 / den, e2 / den, r1, r2)):
        info = jnp.where(lane == col, val, info)
    info_ref[...] = info


def _route(x2, gla, yc, mod3, mod_row, gpm, wo, gpre, w_r, *, tm):
    n, d = x2.shape
    row = lambda c: pl.BlockSpec((None, 1, d), lambda i: (mod_row(i), 0, c))
    const = lambda a: pl.BlockSpec(a.shape, lambda i: (0,) * a.ndim)
    tri = jnp.asarray(np.tril(np.ones((tm, tm), np.float32), -1), BF16)
    return pl.pallas_call(
        _route_kernel,
        grid=(n // tm,),
        in_specs=_mix_specs(mod_row, gpm, wo, tm) + [row(3), row(4), const(gpre), const(w_r), const(tri)],
        out_specs=[pl.BlockSpec((tm, d), lambda i: (i, 0)), pl.BlockSpec((tm, HALF), lambda i: (i, 0)),
                   pl.BlockSpec((tm, LANES), lambda i: (i, 0)), pl.BlockSpec((1, LANES), lambda i: (0, 0))],
        out_shape=[jax.ShapeDtypeStruct((n, d), F32), jax.ShapeDtypeStruct((n, HALF), jnp.uint32),
                   jax.ShapeDtypeStruct((n, LANES), F32), jax.ShapeDtypeStruct((1, LANES), F32)],
        scratch_shapes=[pltpu.VMEM((1, LANES), F32)],
        compiler_params=_cparams(("arbitrary",)),
        name="moe_route",
    )(x2, gla, yc, mod3, gpm, wo, mod3, mod3, gpre, w_r, tri)


def _row_copy(src, s, dst, t, sem):
    return pltpu.make_async_copy(src.at[pl.ds(s, 1), :], dst.at[pl.ds(t, 1), :], sem)


DMA_UNROLL = 8


def _dispatch_kernel(pos_ref, hp_ref, xs_in_ref, xs_ref, sem, *, tm):
    del xs_in_ref

    def start(t, carry):
        _row_copy(hp_ref, t, xs_ref, pos_ref[0, 2 * t], sem).start()
        _row_copy(hp_ref, t, xs_ref, pos_ref[0, 2 * t + 1], sem).start()
        return carry

    def wait(t, carry):
        _row_copy(hp_ref, 0, xs_ref, 0, sem).wait()
        return carry

    lax.fori_loop(0, tm, start, 0, unroll=DMA_UNROLL)
    lax.fori_loop(0, 2 * tm, wait, 0, unroll=DMA_UNROLL)


def _dispatch(pos3, hp, n_sorted, *, tm):
    n = hp.shape[0]
    xs0 = jnp.zeros((n_sorted, HALF), jnp.uint32)
    return pl.pallas_call(
        functools.partial(_dispatch_kernel, tm=tm),
        grid=(n // tm,),
        in_specs=[pl.BlockSpec((None, 1, 2 * tm), lambda i: (i, 0, 0), memory_space=pltpu.SMEM),
                  pl.BlockSpec((tm, HALF), lambda i: (i, 0)), pl.BlockSpec(memory_space=pl.ANY)],
        out_specs=pl.BlockSpec(memory_space=pl.ANY),
        out_shape=jax.ShapeDtypeStruct((n_sorted, HALF), jnp.uint32),
        scratch_shapes=[pltpu.SemaphoreType.DMA],
        input_output_aliases={2: 0},
        compiler_params=pltpu.CompilerParams(dimension_semantics=("arbitrary",), has_side_effects=True),
        name="moe_dispatch",
    )(pos3, hp, xs0)


def _expert_kernel(te_ref, nu_ref, xs_ref, w1_ref, w3_ref, w2_ref, i1_ref, i3_ref, i2_ref, ys_ref, *, tf):
    del te_ref

    @pl.when(pl.program_id(0) < nu_ref[0])
    def _():
        xw = xs_ref[...]
        lo = pltpu.bitcast(xw << 16, F32)
        hi = pltpu.bitcast(xw & jnp.uint32(HI_MASK), F32)
        inv = [r[:, :1] for r in (i1_ref[...], i3_ref[...], i2_ref[...])]
        ys_ref[...] = _pack_rows(_swiglu_f8([lo, hi], w1_ref, w3_ref, w2_ref, *inv, tf))


def _experts(tile_expert, n_used, xs, w1, w3, w2, *, tmg, tf):
    n_sorted = xs.shape[0]
    _, d, ff = w1.shape
    (q1, i1), (q3, i3), (q2, i2) = _quantize(w1), _quantize(w3), _quantize(w2)
    blk = lambda j, te, nu: (jnp.minimum(j, nu[0] - 1), 0)
    held = lambda r, c: pl.BlockSpec((None, r, c), lambda j, te, nu: (te[j], 0, 0), pipeline_mode=pl.Buffered(1))
    grid_spec = pltpu.PrefetchScalarGridSpec(
        num_scalar_prefetch=2,
        grid=(n_sorted // tmg,),
        in_specs=[pl.BlockSpec((tmg, HALF), blk), held(d, ff), held(d, ff), held(ff, d),
                  held(1, LANES), held(1, LANES), held(1, LANES)],
        out_specs=pl.BlockSpec((tmg, HALF), blk),
    )
    return pl.pallas_call(
        functools.partial(_expert_kernel, tf=tf),
        grid_spec=grid_spec,
        out_shape=jax.ShapeDtypeStruct((n_sorted, HALF), jnp.uint32),
        compiler_params=_cparams(("arbitrary",)),
        name="moe_experts",
    )(tile_expert, n_used, xs, q1, q3, q2, i1, i3, i2)


def _combine_kernel(pos_ref, x_ref, gt_ref, gpost_ref, info_ref, ys_ref, o_ref, buf_ref, sems, *, tm):
    i = pl.program_id(0)
    slot = i & 1

    @pl.when(i < pl.num_programs(0) - 1)
    def _():
        def start(t, carry):
            _row_copy(ys_ref, pos_ref[0, 2 * t], buf_ref.at[slot, 0], t, sems.at[slot]).start()
            _row_copy(ys_ref, pos_ref[0, 2 * t + 1], buf_ref.at[slot, 1], t, sems.at[slot]).start()
            return carry

        lax.fori_loop(0, tm, start, 0, unroll=DMA_UNROLL)

    @pl.when(i > 0)
    def _():
        prev = 1 - slot

        def wait(t, carry):
            _row_copy(ys_ref, 0, buf_ref.at[prev, 0], 0, sems.at[prev]).wait()
            return carry

        lax.fori_loop(0, 2 * tm, wait, 0, unroll=DMA_UNROLL)
        info = info_ref[...]
        lane = lax.broadcasted_iota(jnp.int32, info.shape, 1)
        wt1 = jnp.sum(jnp.where(lane == 2, info, 0.0), axis=-1, keepdims=True)
        wt2 = jnp.sum(jnp.where(lane == 3, info, 0.0), axis=-1, keepdims=True)
        y = wt1 * _unpack_rows(buf_ref[prev, 0]) + wt2 * _unpack_rows(buf_ref[prev, 1])
        o_ref[...] = x_ref[...] + gt_ref[...] * _rms(y, gpost_ref[...])


def _combine(pos3, x2, mod3, mod_row, gpost, info, ys, *, tm):
    n, d = x2.shape
    nt = n // tm
    done = lambda i: jnp.maximum(i - 1, 0)
    tile = pl.BlockSpec((tm, d), lambda i: (done(i), 0))
    return pl.pallas_call(
        functools.partial(_combine_kernel, tm=tm),
        grid=(nt + 1,),
        in_specs=[pl.BlockSpec((None, 1, 2 * tm), lambda i: (jnp.minimum(i, nt - 1), 0, 0),
                               memory_space=pltpu.SMEM),
                  tile, pl.BlockSpec((None, 1, d), lambda i: (mod_row(done(i)), 0, 5)),
                  pl.BlockSpec(gpost.shape, lambda i: (0, 0)),
                  pl.BlockSpec((tm, LANES), lambda i: (done(i), 0)), pl.BlockSpec(memory_space=pl.ANY)],
        out_specs=tile,
        out_shape=jax.ShapeDtypeStruct((n, d), F32),
        scratch_shapes=[pltpu.VMEM((2, 2, tm, HALF), jnp.uint32), pltpu.SemaphoreType.DMA((2,))],
        compiler_params=_cparams(("arbitrary",)),
        name="moe_combine",
    )(pos3, x2, mod3, gpost, info, ys)


def _moe_mix(x2, gla, yc, mod3, mod_row, gpm, wo, gpre, gpost, w_r, w1, w3, w2, *, tm, tmg, tf):
    n = x2.shape[0]
    x2, hp, info, cnt = _route(x2, gla, yc, mod3, mod_row, gpm, wo, gpre, w_r, tm=tm)
    counts = cnt[0, :N_EXPERTS].astype(jnp.int32)
    padded = ((counts + tmg - 1) // tmg) * tmg
    ends = jnp.cumsum(padded)
    starts = ends - padded
    n_tiles = (2 * n) // tmg + N_EXPERTS
    picks = info[:, 0:2].astype(jnp.int32)
    pos = starts[picks] + info[:, 4:6].astype(jnp.int32)
    pos3 = pos.reshape(n // tm, 1, 2 * tm)
    tile_expert = jnp.sum(jnp.arange(n_tiles, dtype=jnp.int32)[:, None] * tmg >= ends[None, :], axis=1)
    n_used = (ends[-1] // tmg).astype(jnp.int32).reshape(1)
    tile_expert = jnp.minimum(tile_expert, tile_expert[jnp.maximum(n_used[0] - 1, 0)]).astype(jnp.int32)
    xs = _dispatch(pos3, hp, n_tiles * tmg, tm=tm)
    ys = _experts(tile_expert, n_used, xs, w1, w3, w2, tmg=tmg, tf=tf)
    return _combine(pos3, x2, mod3, mod_row, gpost, info, ys, tm=tm)


def kernel(x, c, ctx, c_ctx, w_mod, b_mod, g_mix_pre, g_mix_post, w_in, w_decay, b_decay, gla_norm,
           conv_w, w_out, g_ffn_pre, g_ffn_post, w1, w3, w2, w_router, e_w1, e_w3, e_w2):
    bsz, seq, d = x.shape
    ctx_len = ctx.shape[1]
    depth = w_mod.shape[0]
    assert d == D_MODEL and seq % (4 * CHUNK) == 0 and ctx_len % CHUNK == 0 and bsz + 1 <= MOD_ROWS

    c_all = jnp.zeros((MOD_ROWS, d), F32).at[:bsz].set(c).at[bsz].set(c_ctx)
    mod = _modulation(c_all, w_mod, b_mod)
    consts = _gla_constants()

    tm_x = 512
    tm_p = 1024
    tm_c = ctx_len
    assert seq % tm_p == 0 and seq % tm_x == 0 and ctx_len & (ctx_len - 1) == 0
    x_row = lambda i: i // (seq // tm_x)
    c_row = lambda i: bsz
    x2 = x.reshape(bsz * seq, d)
    xc2 = ctx.reshape(bsz * ctx_len, d)
    zero_state = jnp.zeros((bsz, 2, 2 * GLA_DV, 2 * GLA_DK), F32)
    row2 = lambda a: a.reshape(1, -1)

    o_q, o_k, o_v = 0, GLA_KW, 2 * GLA_KW
    o_g = o_v + GLA_VW
    o_a = o_g + GLA_VW
    o_c = o_a + 2 * DECAY_RANK

    for i in range(depth):
        last = i == depth - 1
        mod3 = mod[i].reshape(MOD_ROWS, 1, 6 * d)
        wi = w_in[i]
        wm = wi[:, :o_a].astype(BF16)
        wa = jnp.zeros((d, LANES), F32).at[:, :2 * DECAY_RANK].set(wi[:, o_a:o_c]).astype(BF16)
        wc = wi[:, o_c:].astype(BF16)
        wd = jnp.zeros((LANES, 2 * GLA_KW), F32)
        wd = wd.at[:DECAY_RANK, :GLA_KW].set(w_decay[i, 0]).at[DECAY_RANK:2 * DECAY_RANK, GLA_KW:].set(w_decay[i, 1])
        bd = b_decay[i].reshape(1, 2 * GLA_KW)
        wo = w_out[i].astype(BF16)
        gain = row2(gla_norm[i])

        def mix(tokens, mod_row, tm, seg, nseq, slen, s0f, s0b, cps):
            q, k, v, gate, gf, gb, yc = _project(tokens, mod3, mod_row, row2(g_mix_pre[i]), wm, wa, wc, wd, bd,
                                                 conv_w[i], tm=tm, seg=seg)
            r3 = lambda a: a.reshape(nseq, slen, a.shape[-1])
            o, sf, sb = _gla(r3(q), r3(k), r3(v), r3(gf), r3(gb), r3(gate), gain, s0f, s0b, consts, cps=cps)
            return o.reshape(nseq * slen, GLA_VW), yc, sf, sb

        o_c_, yc_c, s_f, s_b = mix(xc2, c_row, tm_c, ctx_len, bsz, ctx_len, zero_state, zero_state,
                                   ctx_len // CHUNK)
        o_x, yc_x, _, _ = mix(x2, lambda t: t // (seq // tm_p), tm_p, GRID_W, bsz, seq, s_f, s_b, 4)

        j = i // 2
        common = dict(gpm=row2(g_mix_post[i]), wo=wo, gpre=row2(g_ffn_pre[i]), gpost=row2(g_ffn_post[i]))
        if i % 2 == 0:
            ffn = functools.partial(_dense_mix, w1=w1[j].astype(BF16), w3=w3[j].astype(BF16),
                                    w2=w2[j].astype(BF16), tf=4 * MXU_TILE, **common)
        else:
            w_r = jnp.zeros((d, LANES), F32).at[:, :N_EXPERTS].set(w_router[j])
            ffn = functools.partial(_moe_mix, w_r=w_r, w1=e_w1[j], w3=e_w3[j], w2=e_w2[j], tmg=512,
                                    tf=7 * MXU_TILE, **common)
        x2 = ffn(x2, o_x, yc_x, mod3, x_row, tm=tm_x)
        if not last:
            xc2 = ffn(xc2, o_c_, yc_c, mod3, c_row, tm=tm_c)
    return x2.reshape(bsz, seq, d)
```

```python
import functools

import numpy as np
import jax
import jax.numpy as jnp
from jax import lax
from jax.experimental import pallas as pl
from jax.experimental.pallas import tpu as pltpu

F32 = jnp.float32
BF16 = jnp.bfloat16
HIGHEST = lax.Precision.HIGHEST

D_MODEL = 1024
GLA_HEADS = 4
GLA_DK = 64
GLA_DV = 128
GLA_KW = GLA_HEADS * GLA_DK
GLA_VW = GLA_HEADS * GLA_DV
DECAY_RANK = 16
GATE_NORMALIZER = 16.0
CHUNK = 64
CONV_CH = D_MODEL - GLA_VW
GRID_W = 64
N_EXPERTS = 8
EPS = 1e-6
LANES = 128
N_LEVELS = 6
MM_LEVELS = N_LEVELS - 1
MOD_ROWS = 24
VMEM_LIMIT = 52 * 1024 * 1024


def _cparams(sem):
    return pltpu.CompilerParams(dimension_semantics=sem, vmem_limit_bytes=VMEM_LIMIT)


def _rms(x, gain):
    return x * lax.rsqrt(jnp.mean(x * x, axis=-1, keepdims=True) + EPS) * gain


def _silu(x):
    return x / (1.0 + jnp.exp(-x))


def _dot_split(a, b):
    ah = a.astype(BF16)
    al = (a - ah.astype(F32)).astype(BF16)
    bh = b.astype(BF16)
    bl = (b - bh.astype(F32)).astype(BF16)
    dot = functools.partial(jnp.dot, preferred_element_type=F32)
    return dot(ah, bh) + (dot(ah, bl) + dot(al, bh))


def _mod_kernel(c_ref, w_ref, b_ref, o_ref):
    s = _silu(c_ref[...])
    o_ref[...] = jnp.dot(s, w_ref[...], precision=HIGHEST, preferred_element_type=F32) + b_ref[...]


def _modulation(c_all, w_mod, b_mod):
    depth, d, n = w_mod.shape
    tn = 1536
    return pl.pallas_call(
        _mod_kernel,
        grid=(depth, n // tn),
        in_specs=[
            pl.BlockSpec((MOD_ROWS, d), lambda i, j: (0, 0)),
            pl.BlockSpec((None, d, tn), lambda i, j: (i, 0, j)),
            pl.BlockSpec((None, 1, tn), lambda i, j: (i, 0, j)),
        ],
        out_specs=pl.BlockSpec((None, MOD_ROWS, tn), lambda i, j: (i, 0, j)),
        out_shape=jax.ShapeDtypeStruct((depth, MOD_ROWS, n), F32),
        compiler_params=_cparams(("parallel", "parallel")),
        name="modulation",
    )(c_all, w_mod, b_mod.reshape(depth, 1, n))


def _proj_kernel(x_ref, sh_ref, sc_ref, g_ref, wm_ref, wa_ref, wc_ref, wd_ref, bd_ref, cw_ref,
                 q_ref, k_ref, v_ref, gate_ref, gf_ref, gb_ref, yc_ref, *, seg, sub):
    for r0 in range(0, x_ref.shape[0], sub):
        rows = slice(r0, r0 + sub)
        h = _rms(x_ref[rows, :], g_ref[...]) * (1.0 + sc_ref[...]) + sh_ref[...]
        hb = h.astype(BF16)
        p = jnp.dot(hb, wm_ref[...], preferred_element_type=F32)
        q_ref[rows, :] = (p[:, :GLA_KW] * (GLA_DK ** -0.5)).astype(BF16)
        k_ref[rows, :] = p[:, GLA_KW:2 * GLA_KW].astype(BF16)
        v_ref[rows, :] = p[:, 2 * GLA_KW:2 * GLA_KW + GLA_VW].astype(BF16)
        gate_ref[rows, :] = p[:, 2 * GLA_KW + GLA_VW:].astype(BF16)
        pa = jnp.dot(hb, wa_ref[...], preferred_element_type=F32)
        xd = _dot_split(pa, wd_ref[...]) + bd_ref[...]
        ls = (jnp.minimum(xd, 0.0) - jnp.log1p(jnp.exp(-jnp.abs(xd)))) * (1.0 / GATE_NORMALIZER)
        gf_ref[rows, :] = ls[:, :GLA_KW]
        gb_ref[rows, :] = ls[:, GLA_KW:]
        pc = jnp.dot(hb, wc_ref[...], preferred_element_type=F32)
        u = pc[:, CONV_CH:2 * CONV_CH] * pc[:, 2 * CONV_CH:]
        row = lax.broadcasted_iota(jnp.int32, u.shape, 0) & (seg - 1)
        u_prev = jnp.where(row == 0, 0.0, pltpu.roll(u, 1, 0))
        u_next = jnp.where(row == seg - 1, 0.0, pltpu.roll(u, sub - 1, 0))
        cw = cw_ref[...]
        yc = pc[:, :CONV_CH] * (cw[0:1] * u_prev + cw[1:2] * u + cw[2:3] * u_next)
        yc_ref[rows, :] = yc.astype(BF16)


def _project(x2, mod3, mod_row, gain, wm, wa, wc, wd, bd, cw, *, tm, seg):
    n = x2.shape[0]
    d = D_MODEL
    row = lambda c: pl.BlockSpec((None, 1, d), lambda i: (mod_row(i), 0, c))
    const = lambda a: pl.BlockSpec(a.shape, lambda i: (0,) * a.ndim, pipeline_mode=pl.Buffered(1))
    tile = lambda w: pl.BlockSpec((tm, w), lambda i: (i, 0))
    shapes = [(GLA_KW, BF16), (GLA_KW, BF16), (GLA_VW, BF16), (GLA_VW, BF16),
              (GLA_KW, F32), (GLA_KW, F32), (CONV_CH, BF16)]
    return pl.pallas_call(
        functools.partial(_proj_kernel, seg=seg, sub=min(tm, 512)),
        grid=(n // tm,),
        in_specs=[tile(d), row(0), row(1), const(gain), const(wm), const(wa), const(wc),
                  const(wd), const(bd), const(cw)],
        out_specs=[tile(w) for w, _ in shapes],
        out_shape=[jax.ShapeDtypeStruct((n, w), t) for w, t in shapes],
        compiler_params=_cparams(("parallel",)),
        name="in_proj",
    )(x2, mod3, mod3, gain, wm, wa, wc, wd, bd, cw)


def _gla_constants():
    c = CHUNK
    main = np.zeros((2 * MM_LEVELS + 2, c, 2 * c), np.float32)
    lmask = np.zeros((N_LEVELS + 1, c, c), np.float32)
    for lvl in range(N_LEVELS):
        s = c >> (lvl + 1)
        for i in range(c):
            mid = (i // (2 * s)) * 2 * s + s
            if lvl < MM_LEVELS and i >= mid:
                main[2 * lvl, i, mid:i + 1] = 1.0
                main[2 * lvl + 1, i, c + mid:c + i] = 1.0
            elif lvl < MM_LEVELS:
                main[2 * lvl, i, c + i:c + mid] = 1.0
                main[2 * lvl + 1, i, i + 1:mid] = 1.0
            for j in range(c):
                same = (j // (2 * s)) == (i // (2 * s))
                lmask[lvl, i, j] = float(same and ((i >= mid) != (j >= mid)))
    lmask[N_LEVELS] = 2.0 * np.eye(c)
    bwd = np.zeros((2, c, c), np.float32)
    for i in range(c):
        main[2 * MM_LEVELS, i, :i + 1] = 1.0
        main[2 * MM_LEVELS + 1, i, i + 1:c] = 1.0
        bwd[0, i, i:] = 1.0
        bwd[1, i, :i] = 1.0
    main = main.reshape(-1, 2 * c)
    bwd = bwd.reshape(-1, c)
    lmask = np.concatenate([lmask, lmask], axis=-1)
    return jnp.asarray(main, BF16), jnp.asarray(bwd, BF16), jnp.asarray(lmask, F32)


def _sum_dot(n_ref, g, parts):
    gb = g.astype(BF16)
    rows = n_ref.shape[0] // parts
    outs = [jnp.dot(n_ref[p * rows:(p + 1) * rows, :], gb, preferred_element_type=F32) for p in range(parts)]
    per = rows // CHUNK
    return lambda b: outs[b // per][(b % per) * CHUNK:(b % per + 1) * CHUNK]


_NT = (((1,), (1,)), ((), ()))
_TN = (((0,), (0,)), ((), ()))


def _gla_kernel(qf_ref, kf_ref, vf_ref, gff_ref, gbf_ref, qb_ref, kb_ref, vb_ref, gbb_ref,
                gate_ref, gain_ref, s0f_ref, s0b_ref, nmain_ref, nbwd_ref, lmask_ref,
                out_ref, sf_out_ref, sb_out_ref, acc_ref, sf_ref, sb_ref, *, cps):
    n = pl.program_id(1)
    nb = pl.num_programs(1)
    c = CHUNK
    pw = 2 * GLA_DK
    vw = 2 * GLA_DV

    @pl.when(n == 0)
    def _():
        acc_ref[...] = jnp.zeros_like(acc_ref)
        sf_ref[...] = s0f_ref[...]
        sb_ref[...] = s0b_ref[...]

    klane = lax.broadcasted_iota(jnp.int32, (c, GLA_KW), 1)
    k_even = (klane & (pw - 1)) < GLA_DK
    vlane = lax.broadcasted_iota(jnp.int32, (c, vw), 1)
    v_low = vlane < GLA_DV
    srow = lax.broadcasted_iota(jnp.int32, (vw, pw), 0)
    scol = lax.broadcasted_iota(jnp.int32, (vw, pw), 1)
    s_diag = (srow < GLA_DV) == (scol < GLA_DK)
    zero_b = jnp.zeros((), BF16)

    def inter(qi, ki, dec, v, s_ref):
        outs = []
        for p in range(2):
            st = s_ref[p]
            outs.append(lax.dot_general(qi[:, p * pw:(p + 1) * pw], st.astype(BF16), _NT,
                                        preferred_element_type=F32))
            upd = lax.dot_general(v[:, p * vw:(p + 1) * vw], ki[:, p * pw:(p + 1) * pw], _TN,
                                  preferred_element_type=F32)
            s_ref[p] = st * dec[:, p * pw:(p + 1) * pw] + jnp.where(s_diag, upd, 0.0)
        return jnp.concatenate(outs, axis=1)

    odd_row = (lax.broadcasted_iota(jnp.int32, (c, GLA_KW), 0) & 1) == 1

    def fwd_chunk(j):
        r0 = pl.multiple_of(j * c, c)
        rows = pl.ds(r0, c)
        qb16 = qf_ref[rows, :]
        kb16 = kf_ref[rows, :]
        v = vf_ref[rows, :]
        q = qb16.astype(F32)
        k = kb16.astype(F32)
        gf = gff_ref[rows, :]
        gb = gbf_ref[rows, :]
        e = _sum_dot(nmain_ref, jnp.concatenate([gf, gb], axis=0), 2)
        att = [jnp.zeros((c, pw), F32), jnp.zeros((c, pw), F32)]
        for lvl in range(N_LEVELS + 1):
            if lvl < MM_LEVELS:
                qt = (q * jnp.exp(e(2 * lvl))).astype(BF16)
                kt = (k * jnp.exp(e(2 * lvl + 1))).astype(BF16)
            elif lvl < N_LEVELS:
                qt, kt = (q * jnp.exp(jnp.where(odd_row, gf, gb))).astype(BF16), kb16
            else:
                qt, kt = qb16, kb16
            ke = jnp.where(k_even, kt, zero_b)
            ko = jnp.where(k_even, zero_b, kt)
            m = lmask_ref[lvl]
            for p in range(2):
                kbd = jnp.concatenate([ke[:, p * pw:(p + 1) * pw], ko[:, p * pw:(p + 1) * pw]], axis=0)
                att[p] = att[p] + m * lax.dot_general(qt[:, p * pw:(p + 1) * pw], kbd, _NT,
                                                      preferred_element_type=F32)
        outs = []
        for p in range(2):
            vp = v[:, p * vw:(p + 1) * vw]
            vbd = jnp.concatenate([jnp.where(v_low, vp, zero_b), jnp.where(v_low, zero_b, vp)], axis=0)
            outs.append(jnp.dot(att[p].astype(BF16), vbd, preferred_element_type=F32))
        o = jnp.concatenate(outs, axis=1)
        cum = e(2 * MM_LEVELS)
        qi = (q * jnp.exp(cum)).astype(BF16)
        ki = (k * jnp.exp(e(2 * MM_LEVELS + 1))).astype(BF16)
        o = o + inter(qi, ki, jnp.exp(cum[c - 1:c]), v, sf_ref)
        a0 = pl.multiple_of(n * (cps * c) + r0, c)
        acc_ref[pl.ds(a0, c), :] += o

    def bwd_chunk(j):
        r0 = pl.multiple_of((cps - 1 - j) * c, c)
        rows = pl.ds(r0, c)
        q = qb_ref[rows, :].astype(F32)
        k = kb_ref[rows, :].astype(F32)
        v = vb_ref[rows, :]
        e = _sum_dot(nbwd_ref, gbb_ref[rows, :], 1)
        cum = e(0)
        qi = (q * jnp.exp(cum)).astype(BF16)
        ki = (k * jnp.exp(e(1))).astype(BF16)
        o = inter(qi, ki, jnp.exp(cum[0:1]), v, sb_ref)
        a0 = pl.multiple_of((nb - 1 - n) * (cps * c) + r0, c)
        acc_ref[pl.ds(a0, c), :] += o

    def both(j, carry):
        fwd_chunk(j)
        bwd_chunk(j)
        return carry

    lax.fori_loop(0, cps, both, 0, unroll=True)

    @pl.when(n == nb - 1)
    def _():
        sf_out_ref[...] = sf_ref[...]
        sb_out_ref[...] = sb_ref[...]
        gain = gain_ref[...]
        rb = 256

        def gate_rows(i, carry):
            rows = pl.ds(pl.multiple_of(i * rb, rb), rb)
            o = acc_ref[rows, :]
            gt = gate_ref[rows, :].astype(F32)
            normed = jnp.concatenate(
                [_rms(o[:, h * GLA_DV:(h + 1) * GLA_DV], gain) for h in range(GLA_HEADS)], axis=1)
            out_ref[rows, :] = (normed * _silu(gt)).astype(BF16)
            return carry

        lax.fori_loop(0, acc_ref.shape[0] // rb, gate_rows, 0)


def _gla(q, k, v, gf, gb, gate, gain, s0f, s0b, consts, *, cps):
    b, l, _ = q.shape
    t = cps * CHUNK
    nb = l // t
    nmain, nbwd, lmask = consts
    fwd = lambda w: pl.BlockSpec((None, t, w), lambda i, n: (i, n, 0))
    bwd = lambda w: pl.BlockSpec((None, t, w), lambda i, n: (i, nb - 1 - n, 0))
    whole = lambda w: pl.BlockSpec((None, l, w), lambda i, n: (i, 0, 0))
    state = pl.BlockSpec((None, 2, 2 * GLA_DV, 2 * GLA_DK), lambda i, n: (i, 0, 0, 0))
    const = lambda a: pl.BlockSpec(a.shape, lambda i, n: (0,) * a.ndim)
    sshape = jax.ShapeDtypeStruct((b, 2, 2 * GLA_DV, 2 * GLA_DK), F32)
    return pl.pallas_call(
        functools.partial(_gla_kernel, cps=cps),
        grid=(b, nb),
        in_specs=[fwd(GLA_KW), fwd(GLA_KW), fwd(GLA_VW), fwd(GLA_KW), fwd(GLA_KW),
                  bwd(GLA_KW), bwd(GLA_KW), bwd(GLA_VW), bwd(GLA_KW),
                  whole(GLA_VW), const(gain), state, state, const(nmain), const(nbwd), const(lmask)],
        out_specs=[whole(GLA_VW), state, state],
        out_shape=[jax.ShapeDtypeStruct((b, l, GLA_VW), BF16), sshape, sshape],
        scratch_shapes=[pltpu.VMEM((l, GLA_VW), F32),
                        pltpu.VMEM((2, 2 * GLA_DV, 2 * GLA_DK), F32),
                        pltpu.VMEM((2, 2 * GLA_DV, 2 * GLA_DK), F32)],
        compiler_params=_cparams(("parallel", "arbitrary")),
        name="gla_scan",
    )(q, k, v, gf, gb, q, k, v, gb, gate, gain, s0f, s0b, nmain, nbwd, lmask)


MXU_TILE = 256


def _ff_chunks(ff, tf):
    return [(f0, min(f0 + tf, ff)) for f0 in range(0, ff, tf)]


def _swiglu(xs, w1_ref, w3_ref, w2_ref, tf):
    k = xs[0].shape[1]
    acc = None
    for f0, f1 in _ff_chunks(w1_ref.shape[1], tf):
        a = sum(jnp.dot(x, w1_ref[r * k:(r + 1) * k, f0:f1], preferred_element_type=F32) for r, x in enumerate(xs))
        b = sum(jnp.dot(x, w3_ref[r * k:(r + 1) * k, f0:f1], preferred_element_type=F32) for r, x in enumerate(xs))
        y = jnp.dot((_silu(a) * b).astype(BF16), w2_ref[f0:f1, :], preferred_element_type=F32)
        acc = y if acc is None else acc + y
    return acc


F8 = jnp.float8_e4m3fn
F8_PEAK = 256.0
TINY = 1e-30
LOG2E = 1.4426950408889634


def _quant_kernel(w_ref, q_ref, inv_ref, *, rb):
    nblk = w_ref.shape[0] // rb

    def peak(i, m):
        blk = jnp.abs(w_ref[pl.ds(pl.multiple_of(i * rb, rb), rb), :])
        return jnp.maximum(m, jnp.max(blk, axis=0, keepdims=True))

    colmax = lax.fori_loop(0, nblk, peak, jnp.zeros((1, w_ref.shape[1]), F32))
    amax = jnp.maximum(jnp.max(colmax, axis=1, keepdims=True), TINY)
    scale = F8_PEAK / amax

    def cast(i, carry):
        rows = pl.ds(pl.multiple_of(i * rb, rb), rb)
        q_ref[rows, :] = (w_ref[rows, :] * scale).astype(F8)
        return carry

    lax.fori_loop(0, nblk, cast, 0)
    inv_ref[...] = jnp.broadcast_to(amax * (1.0 / F8_PEAK), inv_ref.shape)


def _quantize(w):
    ne, r, c = w.shape
    return pl.pallas_call(
        functools.partial(_quant_kernel, rb=128),
        grid=(ne,),
        in_specs=[pl.BlockSpec((None, r, c), lambda e: (e, 0, 0))],
        out_specs=[pl.BlockSpec((None, r, c), lambda e: (e, 0, 0)), pl.BlockSpec((None, 1, LANES), lambda e: (e, 0, 0))],
        out_shape=[jax.ShapeDtypeStruct((ne, r, c), F8), jax.ShapeDtypeStruct((ne, 1, LANES), F32)],
        compiler_params=_cparams(("parallel",)),
        name="quantize_weights",
    )(w)


def _row_scale(parts):
    amax = functools.reduce(jnp.maximum, [jnp.max(jnp.abs(p), axis=1, keepdims=True) for p in parts])
    amax = jnp.maximum(amax, TINY)
    return F8_PEAK / amax, amax * (1.0 / F8_PEAK)


def _swiglu_f8(xs, w1_ref, w3_ref, w2_ref, inv1, inv3, inv2, tf):
    k = xs[0].shape[1]
    sx, ix = _row_scale(xs)
    x8 = [(x * sx).astype(F8) for x in xs]
    acc = None
    for f0, f1 in _ff_chunks(w1_ref.shape[1], tf):
        a = sum(jnp.dot(x, w1_ref[r * k:(r + 1) * k, f0:f1], preferred_element_type=F32) for r, x in enumerate(x8))
        b = sum(jnp.dot(x, w3_ref[r * k:(r + 1) * k, f0:f1], preferred_element_type=F32) for r, x in enumerate(x8))
        c1 = ix * inv1
        m = a * b / (1.0 + jnp.exp2(a * (c1 * -LOG2E)))
        sm, im = _row_scale([m])
        y = (jnp.dot((m * sm).astype(F8), w2_ref[f0:f1, :], preferred_element_type=F32)
             * (im * c1 * (ix * inv3) * inv2))
        acc = y if acc is None else acc + y
    return acc


def _token_mix_residual(x_ref, gla_ref, yc_ref, gt_ref, gp_ref, wo_ref, rows=slice(None)):
    y = (jnp.dot(gla_ref[rows, :], wo_ref[:GLA_VW, :], preferred_element_type=F32)
         + jnp.dot(yc_ref[rows, :], wo_ref[GLA_VW:, :], preferred_element_type=F32))
    return x_ref[rows, :] + gt_ref[...] * _rms(y, gp_ref[...])


def _mix_specs(mod_row, gpm, wo, tm):
    d = D_MODEL
    return [pl.BlockSpec((tm, d), lambda i: (i, 0)), pl.BlockSpec((tm, GLA_VW), lambda i: (i, 0)),
            pl.BlockSpec((tm, CONV_CH), lambda i: (i, 0)),
            pl.BlockSpec((None, 1, d), lambda i: (mod_row(i), 0, 2)),
            pl.BlockSpec(gpm.shape, lambda i: (0, 0)),
            pl.BlockSpec(wo.shape, lambda i: (0, 0), pipeline_mode=pl.Buffered(1))]


def _dense_kernel(x_ref, gla_ref, yc_ref, gt1_ref, gpm_ref, wo_ref, sh_ref, sc_ref, gt_ref, gpre_ref, gpost_ref,
                  w1_ref, w3_ref, w2_ref, o_ref, *, tf, sub):
    for r0 in range(0, x_ref.shape[0], sub):
        rows = slice(r0, r0 + sub)
        x1 = _token_mix_residual(x_ref, gla_ref, yc_ref, gt1_ref, gpm_ref, wo_ref, rows)
        h = _rms(x1, gpre_ref[...]) * (1.0 + sc_ref[...]) + sh_ref[...]
        y = _swiglu([h.astype(BF16)], w1_ref, w3_ref, w2_ref, tf)
        o_ref[rows, :] = x1 + gt_ref[...] * _rms(y, gpost_ref[...])


def _dense_mix(x2, gla, yc, mod3, mod_row, gpm, wo, gpre, gpost, w1, w3, w2, *, tm, tf):
    n, d = x2.shape
    row = lambda c: pl.BlockSpec((None, 1, d), lambda i: (mod_row(i), 0, c))
    const = lambda a: pl.BlockSpec(a.shape, lambda i: (0,) * a.ndim)
    held = lambda a: pl.BlockSpec(a.shape, lambda i: (0,) * a.ndim, pipeline_mode=pl.Buffered(1))
    return pl.pallas_call(
        functools.partial(_dense_kernel, tf=tf, sub=min(tm, 512)),
        grid=(n // tm,),
        in_specs=_mix_specs(mod_row, gpm, wo, tm) + [row(3), row(4), row(5), const(gpre), const(gpost),
                                                     held(w1), held(w3), held(w2)],
        out_specs=pl.BlockSpec((tm, d), lambda i: (i, 0)),
        out_shape=jax.ShapeDtypeStruct((n, d), F32),
        compiler_params=_cparams(("parallel",)),
        name="dense_mix",
    )(x2, gla, yc, mod3, gpm, wo, mod3, mod3, mod3, gpre, gpost, w1, w3, w2)


HALF = D_MODEL // 2
HI_MASK = 0xFFFF0000


def _pack_rows(v):
    bits = pltpu.bitcast(v.astype(BF16).astype(F32), jnp.uint32)
    return (bits[:, :HALF] >> 16) | (bits[:, HALF:] & jnp.uint32(HI_MASK))


def _unpack_rows(w):
    lo = pltpu.bitcast(w << 16, F32)
    hi = pltpu.bitcast(w & jnp.uint32(HI_MASK), F32)
    return jnp.concatenate([lo, hi], axis=1)


def _route_kernel(x_ref, gla_ref, yc_ref, gt1_ref, gpm_ref, wo_ref, sh_ref, sc_ref, gpre_ref, wr_ref, tri_ref,
                  x1_ref, hp_ref, info_ref, cnt_ref, carry_ref):
    i = pl.program_id(0)

    @pl.when(i == 0)
    def _():
        carry_ref[...] = jnp.zeros_like(carry_ref)

    sub = tri_ref.shape[0]
    carry = carry_ref[...]
    for r0 in range(0, x_ref.shape[0], sub):
        rows = slice(r0, r0 + sub)
        x1 = _token_mix_residual(x_ref, gla_ref, yc_ref, gt1_ref, gpm_ref, wo_ref, rows)
        x1_ref[rows, :] = x1
        h = _rms(x1, gpre_ref[...]) * (1.0 + sc_ref[...]) + sh_ref[...]
        hp_ref[rows, :] = _pack_rows(h)
        logits = _dot_split(h, wr_ref[...])
        lane = lax.broadcasted_iota(jnp.int32, logits.shape, 1).astype(F32)
        logits = jnp.where(lane < N_EXPERTS, logits, -jnp.inf)
        m1 = jnp.max(logits, axis=-1, keepdims=True)
        i1 = jnp.min(jnp.where(logits == m1, lane, float(LANES)), axis=-1, keepdims=True)
        rest = jnp.where(lane == i1, -jnp.inf, logits)
        m2 = jnp.max(rest, axis=-1, keepdims=True)
        i2 = jnp.min(jnp.where(rest == m2, lane, float(LANES)), axis=-1, keepdims=True)
        e2 = jnp.exp(m2 - m1)
        den = 1.0 + e2
        pick = jnp.where((lane == i1) | (lane == i2), 1.0, 0.0)
        rank = jnp.dot(tri_ref[...], pick.astype(BF16), preferred_element_type=F32) + carry
        carry = carry + jnp.sum(pick, axis=0, keepdims=True)
        r1 = jnp.sum(jnp.where(lane == i1, rank, 0.0), axis=-1, keepdims=True)
        r2 = jnp.sum(jnp.where(lane == i2, rank, 0.0), axis=-1, keepdims=True)
        info = jnp.zeros_like(logits)
        for col, val in enumerate((i1, i2, 1.0 / den, e2 / den, r1, r2)):
            info = jnp.where(lane == col, val, info)
        info_ref[rows, :] = info
    carry_ref[...] = carry
    cnt_ref[...] = carry


def _route(x2, gla, yc, mod3, mod_row, gpm, wo, gpre, w_r, *, tm):
    n, d = x2.shape
    row = lambda c: pl.BlockSpec((None, 1, d), lambda i: (mod_row(i), 0, c))
    const = lambda a: pl.BlockSpec(a.shape, lambda i: (0,) * a.ndim)
    sub = min(tm, 256)
    tri = jnp.asarray(np.tril(np.ones((sub, sub), np.float32), -1), BF16)
    return pl.pallas_call(
        _route_kernel,
        grid=(n // tm,),
        in_specs=_mix_specs(mod_row, gpm, wo, tm) + [row(3), row(4), const(gpre), const(w_r), const(tri)],
        out_specs=[pl.BlockSpec((tm, d), lambda i: (i, 0)), pl.BlockSpec((tm, HALF), lambda i: (i, 0)),
                   pl.BlockSpec((tm, LANES), lambda i: (i, 0)), pl.BlockSpec((1, LANES), lambda i: (0, 0))],
        out_shape=[jax.ShapeDtypeStruct((n, d), F32), jax.ShapeDtypeStruct((n, HALF), jnp.uint32),
                   jax.ShapeDtypeStruct((n, LANES), F32), jax.ShapeDtypeStruct((1, LANES), F32)],
        scratch_shapes=[pltpu.VMEM((1, LANES), F32)],
        compiler_params=_cparams(("arbitrary",)),
        name="moe_route",
    )(x2, gla, yc, mod3, gpm, wo, mod3, mod3, gpre, w_r, tri)


def _row_copy(src, s, dst, t, sem):
    return pltpu.make_async_copy(src.at[pl.ds(s, 1), :], dst.at[pl.ds(t, 1), :], sem)


DMA_UNROLL = 8


def _dispatch_kernel(pos_ref, hp_ref, xs_in_ref, xs_ref, sem, *, tm):
    del xs_in_ref

    def start(t, carry):
        _row_copy(hp_ref, t, xs_ref, pos_ref[0, 2 * t], sem).start(priority=0)
        _row_copy(hp_ref, t, xs_ref, pos_ref[0, 2 * t + 1], sem).start(priority=1)
        return carry

    def wait(t, carry):
        _row_copy(hp_ref, 0, xs_ref, 0, sem).wait()
        return carry

    lax.fori_loop(0, tm, start, 0, unroll=DMA_UNROLL)
    lax.fori_loop(0, 2 * tm, wait, 0, unroll=DMA_UNROLL)


def _dispatch(pos3, hp, n_sorted, *, tm):
    n = hp.shape[0]
    xs0 = jnp.zeros((n_sorted, HALF), jnp.uint32)
    return pl.pallas_call(
        functools.partial(_dispatch_kernel, tm=tm),
        grid=(n // tm,),
        in_specs=[pl.BlockSpec((None, 1, 2 * tm), lambda i: (i, 0, 0), memory_space=pltpu.SMEM),
                  pl.BlockSpec((tm, HALF), lambda i: (i, 0)), pl.BlockSpec(memory_space=pl.ANY)],
        out_specs=pl.BlockSpec(memory_space=pl.ANY),
        out_shape=jax.ShapeDtypeStruct((n_sorted, HALF), jnp.uint32),
        scratch_shapes=[pltpu.SemaphoreType.DMA],
        input_output_aliases={2: 0},
        compiler_params=pltpu.CompilerParams(dimension_semantics=("arbitrary",), has_side_effects=True),
        name="moe_dispatch",
    )(pos3, hp, xs0)


def _expert_kernel(te_ref, nu_ref, xs_ref, w1_ref, w3_ref, w2_ref, i1_ref, i3_ref, i2_ref, ys_ref, *, tf):
    del te_ref

    @pl.when(pl.program_id(0) < nu_ref[0])
    def _():
        xw = xs_ref[...]
        lo = pltpu.bitcast(xw << 16, F32)
        hi = pltpu.bitcast(xw & jnp.uint32(HI_MASK), F32)
        inv = [r[:, :1] for r in (i1_ref[...], i3_ref[...], i2_ref[...])]
        ys_ref[...] = _pack_rows(_swiglu_f8([lo, hi], w1_ref, w3_ref, w2_ref, *inv, tf))


def _experts(tile_expert, n_used, xs, w1, w3, w2, *, tmg, tf):
    n_sorted = xs.shape[0]
    _, d, ff = w1.shape
    (q1, i1), (q3, i3), (q2, i2) = _quantize(w1), _quantize(w3), _quantize(w2)
    blk = lambda j, te, nu: (jnp.minimum(j, nu[0] - 1), 0)
    held = lambda r, c: pl.BlockSpec((None, r, c), lambda j, te, nu: (te[j], 0, 0), pipeline_mode=pl.Buffered(1))
    grid_spec = pltpu.PrefetchScalarGridSpec(
        num_scalar_prefetch=2,
        grid=(n_sorted // tmg,),
        in_specs=[pl.BlockSpec((tmg, HALF), blk), held(d, ff), held(d, ff), held(ff, d),
                  held(1, LANES), held(1, LANES), held(1, LANES)],
        out_specs=pl.BlockSpec((tmg, HALF), blk),
    )
    return pl.pallas_call(
        functools.partial(_expert_kernel, tf=tf),
        grid_spec=grid_spec,
        out_shape=jax.ShapeDtypeStruct((n_sorted, HALF), jnp.uint32),
        compiler_params=_cparams(("arbitrary",)),
        name="moe_experts",
    )(tile_expert, n_used, xs, q1, q3, q2, i1, i3, i2)


def _combine_kernel(pos_ref, x_ref, gt_ref, gpost_ref, info_ref, ys_ref, o_ref, buf_ref, sems, *, tm):
    i = pl.program_id(0)
    slot = i & 1

    @pl.when(i < pl.num_programs(0) - 1)
    def _():
        def start(t, carry):
            _row_copy(ys_ref, pos_ref[0, 2 * t], buf_ref.at[slot, 0], t, sems.at[slot]).start(priority=0)
            _row_copy(ys_ref, pos_ref[0, 2 * t + 1], buf_ref.at[slot, 1], t, sems.at[slot]).start(priority=1)
            return carry

        lax.fori_loop(0, tm, start, 0, unroll=DMA_UNROLL)

    @pl.when(i > 0)
    def _():
        prev = 1 - slot

        def wait(t, carry):
            _row_copy(ys_ref, 0, buf_ref.at[prev, 0], 0, sems.at[prev]).wait()
            return carry

        lax.fori_loop(0, 2 * tm, wait, 0, unroll=DMA_UNROLL)
        info = info_ref[...]
        lane = lax.broadcasted_iota(jnp.int32, info.shape, 1)
        wt1 = jnp.sum(jnp.where(lane == 2, info, 0.0), axis=-1, keepdims=True)
        wt2 = jnp.sum(jnp.where(lane == 3, info, 0.0), axis=-1, keepdims=True)
        y = wt1 * _unpack_rows(buf_ref[prev, 0]) + wt2 * _unpack_rows(buf_ref[prev, 1])
        o_ref[...] = x_ref[...] + gt_ref[...] * _rms(y, gpost_ref[...])


def _combine(pos3, x2, mod3, mod_row, gpost, info, ys, *, tm):
    n, d = x2.shape
    nt = n // tm
    done = lambda i: jnp.maximum(i - 1, 0)
    tile = pl.BlockSpec((tm, d), lambda i: (done(i), 0))
    return pl.pallas_call(
        functools.partial(_combine_kernel, tm=tm),
        grid=(nt + 1,),
        in_specs=[pl.BlockSpec((None, 1, 2 * tm), lambda i: (jnp.minimum(i, nt - 1), 0, 0),
                               memory_space=pltpu.SMEM),
                  tile, pl.BlockSpec((None, 1, d), lambda i: (mod_row(done(i)), 0, 5)),
                  pl.BlockSpec(gpost.shape, lambda i: (0, 0)),
                  pl.BlockSpec((tm, LANES), lambda i: (done(i), 0)), pl.BlockSpec(memory_space=pl.ANY)],
        out_specs=tile,
        out_shape=jax.ShapeDtypeStruct((n, d), F32),
        scratch_shapes=[pltpu.VMEM((2, 2, tm, HALF), jnp.uint32), pltpu.SemaphoreType.DMA((2,))],
        compiler_params=_cparams(("arbitrary",)),
        name="moe_combine",
    )(pos3, x2, mod3, gpost, info, ys)


def _moe_mix(x2, gla, yc, mod3, mod_row, gpm, wo, gpre, gpost, w_r, w1, w3, w2, *, tm, tmg, tf):
    n = x2.shape[0]
    x2, hp, info, cnt = _route(x2, gla, yc, mod3, mod_row, gpm, wo, gpre, w_r, tm=tm)
    counts = cnt[0, :N_EXPERTS].astype(jnp.int32)
    padded = ((counts + tmg - 1) // tmg) * tmg
    ends = jnp.cumsum(padded)
    starts = ends - padded
    n_tiles = (2 * n) // tmg + N_EXPERTS
    picks = info[:, 0:2].astype(jnp.int32)
    pos = starts[picks] + info[:, 4:6].astype(jnp.int32)
    pos3 = pos.reshape(n // tm, 1, 2 * tm)
    tile_expert = jnp.sum(jnp.arange(n_tiles, dtype=jnp.int32)[:, None] * tmg >= ends[None, :], axis=1)
    n_used = (ends[-1] // tmg).astype(jnp.int32).reshape(1)
    tile_expert = jnp.minimum(tile_expert, tile_expert[jnp.maximum(n_used[0] - 1, 0)]).astype(jnp.int32)
    xs = _dispatch(pos3, hp, n_tiles * tmg, tm=tm)
    ys = _experts(tile_expert, n_used, xs, w1, w3, w2, tmg=tmg, tf=tf)
    return _combine(pos3, x2, mod3, mod_row, gpost, info, ys, tm=tm)


def kernel(x, c, ctx, c_ctx, w_mod, b_mod, g_mix_pre, g_mix_post, w_in, w_decay, b_decay, gla_norm,
           conv_w, w_out, g_ffn_pre, g_ffn_post, w1, w3, w2, w_router, e_w1, e_w3, e_w2):
    bsz, seq, d = x.shape
    ctx_len = ctx.shape[1]
    depth = w_mod.shape[0]
    assert d == D_MODEL and seq % (4 * CHUNK) == 0 and ctx_len % CHUNK == 0 and bsz + 1 <= MOD_ROWS

    c_all = jnp.zeros((MOD_ROWS, d), F32).at[:bsz].set(c).at[bsz].set(c_ctx)
    mod = _modulation(c_all, w_mod, b_mod)
    consts = _gla_constants()

    tm_x = 512
    tm_p = 1024
    tm_c = ctx_len
    assert seq % tm_p == 0 and seq % tm_x == 0 and ctx_len & (ctx_len - 1) == 0
    x_row = lambda i: i // (seq // tm_x)
    c_row = lambda i: bsz
    x2 = x.reshape(bsz * seq, d)
    xc2 = ctx.reshape(bsz * ctx_len, d)
    zero_state = jnp.zeros((bsz, 2, 2 * GLA_DV, 2 * GLA_DK), F32)
    row2 = lambda a: a.reshape(1, -1)

    o_q, o_k, o_v = 0, GLA_KW, 2 * GLA_KW
    o_g = o_v + GLA_VW
    o_a = o_g + GLA_VW
    o_c = o_a + 2 * DECAY_RANK

    for i in range(depth):
        last = i == depth - 1
        mod3 = mod[i].reshape(MOD_ROWS, 1, 6 * d)
        wi = w_in[i]
        wm = wi[:, :o_a].astype(BF16)
        wa = jnp.zeros((d, LANES), F32).at[:, :2 * DECAY_RANK].set(wi[:, o_a:o_c]).astype(BF16)
        wc = wi[:, o_c:].astype(BF16)
        wd = jnp.zeros((LANES, 2 * GLA_KW), F32)
        wd = wd.at[:DECAY_RANK, :GLA_KW].set(w_decay[i, 0]).at[DECAY_RANK:2 * DECAY_RANK, GLA_KW:].set(w_decay[i, 1])
        bd = b_decay[i].reshape(1, 2 * GLA_KW)
        wo = w_out[i].astype(BF16)
        gain = row2(gla_norm[i])

        def mix(tokens, mod_row, tm, seg, nseq, slen, s0f, s0b, cps):
            q, k, v, gate, gf, gb, yc = _project(tokens, mod3, mod_row, row2(g_mix_pre[i]), wm, wa, wc, wd, bd,
                                                 conv_w[i], tm=tm, seg=seg)
            r3 = lambda a: a.reshape(nseq, slen, a.shape[-1])
            o, sf, sb = _gla(r3(q), r3(k), r3(v), r3(gf), r3(gb), r3(gate), gain, s0f, s0b, consts, cps=cps)
            return o.reshape(nseq * slen, GLA_VW), yc, sf, sb

        o_c_, yc_c, s_f, s_b = mix(xc2, c_row, tm_c, ctx_len, bsz, ctx_len, zero_state, zero_state,
                                   ctx_len // CHUNK)
        o_x, yc_x, _, _ = mix(x2, lambda t: t // (seq // tm_p), tm_p, GRID_W, bsz, seq, s_f, s_b, 4)

        j = i // 2
        common = dict(gpm=row2(g_mix_post[i]), wo=wo, gpre=row2(g_ffn_pre[i]), gpost=row2(g_ffn_post[i]))
        if i % 2 == 0:
            ffn = functools.partial(_dense_mix, w1=w1[j].astype(BF16), w3=w3[j].astype(BF16),
                                    w2=w2[j].astype(BF16), tf=4 * MXU_TILE, **common)
        else:
            w_r = jnp.zeros((d, LANES), F32).at[:, :N_EXPERTS].set(w_router[j])
            ffn = functools.partial(_moe_mix, w_r=w_r, w1=e_w1[j], w3=e_w3[j], w2=e_w2[j], tmg=512,
                                    tf=7 * MXU_TILE, **common)
        x2 = ffn(x2, o_x, yc_x, mod3, x_row, tm=tm_x)
        if not last:
            xc2 = ffn(xc2, o_c_, yc_c, mod3, c_row, tm=tm_c)
    return x2.reshape(bsz, seq, d)
```

```python
import functools

import numpy as np
import jax
import jax.numpy as jnp
from jax import lax
from jax.experimental import pallas as pl
from jax.experimental.pallas import tpu as pltpu

F32 = jnp.float32
BF16 = jnp.bfloat16
HIGHEST = lax.Precision.HIGHEST

D_MODEL = 1024
GLA_HEADS = 4
GLA_DK = 64
GLA_DV = 128
GLA_KW = GLA_HEADS * GLA_DK
GLA_VW = GLA_HEADS * GLA_DV
DECAY_RANK = 16
GATE_NORMALIZER = 16.0
CHUNK = 64
CONV_CH = D_MODEL - GLA_VW
GRID_W = 64
N_EXPERTS = 8
EPS = 1e-6
LANES = 128
N_LEVELS = 6
MM_LEVELS = N_LEVELS - 1
MOD_ROWS = 24
VMEM_LIMIT = 52 * 1024 * 1024


def _cparams(sem):
    return pltpu.CompilerParams(dimension_semantics=sem, vmem_limit_bytes=VMEM_LIMIT)


def _rms(x, gain):
    return x * lax.rsqrt(jnp.mean(x * x, axis=-1, keepdims=True) + EPS) * gain


def _silu(x):
    return x / (1.0 + jnp.exp(-x))


def _dot_split(a, b):
    ah = a.astype(BF16)
    al = (a - ah.astype(F32)).astype(BF16)
    bh = b.astype(BF16)
    bl = (b - bh.astype(F32)).astype(BF16)
    dot = functools.partial(jnp.dot, preferred_element_type=F32)
    return dot(ah, bh) + (dot(ah, bl) + dot(al, bh))


def _mod_kernel(c_ref, w_ref, b_ref, o_ref):
    s = _silu(c_ref[...])
    o_ref[...] = jnp.dot(s, w_ref[...], precision=HIGHEST, preferred_element_type=F32) + b_ref[...]


def _modulation(c_all, w_mod, b_mod):
    depth, d, n = w_mod.shape
    tn = 1536
    return pl.pallas_call(
        _mod_kernel,
        grid=(depth, n // tn),
        in_specs=[
            pl.BlockSpec((MOD_ROWS, d), lambda i, j: (0, 0)),
            pl.BlockSpec((None, d, tn), lambda i, j: (i, 0, j)),
            pl.BlockSpec((None, 1, tn), lambda i, j: (i, 0, j)),
        ],
        out_specs=pl.BlockSpec((None, MOD_ROWS, tn), lambda i, j: (i, 0, j)),
        out_shape=jax.ShapeDtypeStruct((depth, MOD_ROWS, n), F32),
        compiler_params=_cparams(("parallel", "parallel")),
        name="modulation",
    )(c_all, w_mod, b_mod.reshape(depth, 1, n))


def _proj_kernel(x_ref, sh_ref, sc_ref, g_ref, wm_ref, wa_ref, wc_ref, wd_ref, bd_ref, cw_ref,
                 q_ref, k_ref, v_ref, gate_ref, gf_ref, gb_ref, yc_ref, *, seg, sub):
    for r0 in range(0, x_ref.shape[0], sub):
        rows = slice(r0, r0 + sub)
        h = _rms(x_ref[rows, :], g_ref[...]) * (1.0 + sc_ref[...]) + sh_ref[...]
        hb = h.astype(BF16)
        p = jnp.dot(hb, wm_ref[...], preferred_element_type=F32)
        q_ref[rows, :] = (p[:, :GLA_KW] * (GLA_DK ** -0.5)).astype(BF16)
        k_ref[rows, :] = p[:, GLA_KW:2 * GLA_KW].astype(BF16)
        v_ref[rows, :] = p[:, 2 * GLA_KW:2 * GLA_KW + GLA_VW].astype(BF16)
        gate_ref[rows, :] = p[:, 2 * GLA_KW + GLA_VW:].astype(BF16)
        pa = jnp.dot(hb, wa_ref[...], preferred_element_type=F32)
        xd = _dot_split(pa, wd_ref[...]) + bd_ref[...]
        ls = (jnp.minimum(xd, 0.0) - jnp.log1p(jnp.exp(-jnp.abs(xd)))) * (1.0 / GATE_NORMALIZER)
        gf_ref[rows, :] = ls[:, :GLA_KW]
        gb_ref[rows, :] = ls[:, GLA_KW:]
        pc = jnp.dot(hb, wc_ref[...], preferred_element_type=F32)
        u = pc[:, CONV_CH:2 * CONV_CH] * pc[:, 2 * CONV_CH:]
        row = lax.broadcasted_iota(jnp.int32, u.shape, 0) & (seg - 1)
        u_prev = jnp.where(row == 0, 0.0, pltpu.roll(u, 1, 0))
        u_next = jnp.where(row == seg - 1, 0.0, pltpu.roll(u, sub - 1, 0))
        cw = cw_ref[...]
        yc = pc[:, :CONV_CH] * (cw[0:1] * u_prev + cw[1:2] * u + cw[2:3] * u_next)
        yc_ref[rows, :] = yc.astype(BF16)


def _project(x2, mod3, mod_row, gain, wm, wa, wc, wd, bd, cw, *, tm, seg):
    n = x2.shape[0]
    d = D_MODEL
    row = lambda c: pl.BlockSpec((None, 1, d), lambda i: (mod_row(i), 0, c))
    const = lambda a: pl.BlockSpec(a.shape, lambda i: (0,) * a.ndim, pipeline_mode=pl.Buffered(1))
    tile = lambda w: pl.BlockSpec((tm, w), lambda i: (i, 0))
    shapes = [(GLA_KW, BF16), (GLA_KW, BF16), (GLA_VW, BF16), (GLA_VW, BF16),
              (GLA_KW, F32), (GLA_KW, F32), (CONV_CH, BF16)]
    return pl.pallas_call(
        functools.partial(_proj_kernel, seg=seg, sub=min(tm, 512)),
        grid=(n // tm,),
        in_specs=[tile(d), row(0), row(1), const(gain), const(wm), const(wa), const(wc),
                  const(wd), const(bd), const(cw)],
        out_specs=[tile(w) for w, _ in shapes],
        out_shape=[jax.ShapeDtypeStruct((n, w), t) for w, t in shapes],
        compiler_params=_cparams(("parallel",)),
        name="in_proj",
    )(x2, mod3, mod3, gain, wm, wa, wc, wd, bd, cw)


def _gla_constants():
    c = CHUNK
    main = np.zeros((2 * MM_LEVELS + 2, c, 2 * c), np.float32)
    lmask = np.zeros((N_LEVELS + 1, c, c), np.float32)
    for lvl in range(N_LEVELS):
        s = c >> (lvl + 1)
        for i in range(c):
            mid = (i // (2 * s)) * 2 * s + s
            if lvl < MM_LEVELS and i >= mid:
                main[2 * lvl, i, mid:i + 1] = 1.0
                main[2 * lvl + 1, i, c + mid:c + i] = 1.0
            elif lvl < MM_LEVELS:
                main[2 * lvl, i, c + i:c + mid] = 1.0
                main[2 * lvl + 1, i, i + 1:mid] = 1.0
            for j in range(c):
                same = (j // (2 * s)) == (i // (2 * s))
                lmask[lvl, i, j] = float(same and ((i >= mid) != (j >= mid)))
    lmask[N_LEVELS] = 2.0 * np.eye(c)
    bwd = np.zeros((2, c, c), np.float32)
    for i in range(c):
        main[2 * MM_LEVELS, i, :i + 1] = 1.0
        main[2 * MM_LEVELS + 1, i, i + 1:c] = 1.0
        bwd[0, i, i:] = 1.0
        bwd[1, i, :i] = 1.0
    main = main.reshape(-1, 2 * c)
    bwd = bwd.reshape(-1, c)
    lmask = np.concatenate([lmask, lmask], axis=-1)
    return jnp.asarray(main, BF16), jnp.asarray(bwd, BF16), jnp.asarray(lmask, F32)


def _sum_dot(n_ref, g, parts):
    gb = g.astype(BF16)
    rows = n_ref.shape[0] // parts
    outs = [jnp.dot(n_ref[p * rows:(p + 1) * rows, :], gb, preferred_element_type=F32) for p in range(parts)]
    per = rows // CHUNK
    return lambda b: outs[b // per][(b % per) * CHUNK:(b % per + 1) * CHUNK]


_NT = (((1,), (1,)), ((), ()))
_TN = (((0,), (0,)), ((), ()))


def _gla_kernel(qf_ref, kf_ref, vf_ref, gff_ref, gbf_ref, qb_ref, kb_ref, vb_ref, gbb_ref,
                gate_ref, gain_ref, s0f_ref, s0b_ref, nmain_ref, nbwd_ref, lmask_ref,
                out_ref, sf_out_ref, sb_out_ref, acc_ref, sf_ref, sb_ref, *, cps):
    n = pl.program_id(1)
    nb = pl.num_programs(1)
    c = CHUNK
    pw = 2 * GLA_DK
    vw = 2 * GLA_DV

    @pl.when(n == 0)
    def _():
        acc_ref[...] = jnp.zeros_like(acc_ref)
        sf_ref[...] = s0f_ref[...]
        sb_ref[...] = s0b_ref[...]

    klane = lax.broadcasted_iota(jnp.int32, (c, GLA_KW), 1)
    k_even = (klane & (pw - 1)) < GLA_DK
    vlane = lax.broadcasted_iota(jnp.int32, (c, vw), 1)
    v_low = vlane < GLA_DV
    srow = lax.broadcasted_iota(jnp.int32, (vw, pw), 0)
    scol = lax.broadcasted_iota(jnp.int32, (vw, pw), 1)
    s_diag = (srow < GLA_DV) == (scol < GLA_DK)
    zero_b = jnp.zeros((), BF16)

    def inter(qi, ki, dec, v, s_ref):
        outs = []
        for p in range(2):
            st = s_ref[p]
            outs.append(lax.dot_general(qi[:, p * pw:(p + 1) * pw], st.astype(BF16), _NT,
                                        preferred_element_type=F32))
            upd = lax.dot_general(v[:, p * vw:(p + 1) * vw], ki[:, p * pw:(p + 1) * pw], _TN,
                                  preferred_element_type=F32)
            s_ref[p] = st * dec[:, p * pw:(p + 1) * pw] + jnp.where(s_diag, upd, 0.0)
        return jnp.concatenate(outs, axis=1)

    odd_row = (lax.broadcasted_iota(jnp.int32, (c, GLA_KW), 0) & 1) == 1

    def fwd_chunk(j):
        r0 = pl.multiple_of(j * c, c)
        rows = pl.ds(r0, c)
        qb16 = qf_ref[rows, :]
        kb16 = kf_ref[rows, :]
        v = vf_ref[rows, :]
        q = qb16.astype(F32)
        k = kb16.astype(F32)
        gf = gff_ref[rows, :]
        gb = gbf_ref[rows, :]
        e = _sum_dot(nmain_ref, jnp.concatenate([gf, gb], axis=0), 2)
        att = [jnp.zeros((c, pw), F32), jnp.zeros((c, pw), F32)]
        for lvl in range(N_LEVELS + 1):
            if lvl < MM_LEVELS:
                qt = (q * jnp.exp(e(2 * lvl))).astype(BF16)
                kt = (k * jnp.exp(e(2 * lvl + 1))).astype(BF16)
            elif lvl < N_LEVELS:
                qt, kt = (q * jnp.exp(jnp.where(odd_row, gf, gb))).astype(BF16), kb16
            else:
                qt, kt = qb16, kb16
            ke = jnp.where(k_even, kt, zero_b)
            ko = jnp.where(k_even, zero_b, kt)
            m = lmask_ref[lvl]
            for p in range(2):
                kbd = jnp.concatenate([ke[:, p * pw:(p + 1) * pw], ko[:, p * pw:(p + 1) * pw]], axis=0)
                att[p] = att[p] + m * lax.dot_general(qt[:, p * pw:(p + 1) * pw], kbd, _NT,
                                                      preferred_element_type=F32)
        outs = []
        for p in range(2):
            vp = v[:, p * vw:(p + 1) * vw]
            vbd = jnp.concatenate([jnp.where(v_low, vp, zero_b), jnp.where(v_low, zero_b, vp)], axis=0)
            outs.append(jnp.dot(att[p].astype(BF16), vbd, preferred_element_type=F32))
        o = jnp.concatenate(outs, axis=1)
        cum = e(2 * MM_LEVELS)
        qi = (q * jnp.exp(cum)).astype(BF16)
        ki = (k * jnp.exp(e(2 * MM_LEVELS + 1))).astype(BF16)
        o = o + inter(qi, ki, jnp.exp(cum[c - 1:c]), v, sf_ref)
        a0 = pl.multiple_of(n * (cps * c) + r0, c)
        acc_ref[pl.ds(a0, c), :] += o

    def bwd_chunk(j):
        r0 = pl.multiple_of((cps - 1 - j) * c, c)
        rows = pl.ds(r0, c)
        q = qb_ref[rows, :].astype(F32)
        k = kb_ref[rows, :].astype(F32)
        v = vb_ref[rows, :]
        e = _sum_dot(nbwd_ref, gbb_ref[rows, :], 1)
        cum = e(0)
        qi = (q * jnp.exp(cum)).astype(BF16)
        ki = (k * jnp.exp(e(1))).astype(BF16)
        o = inter(qi, ki, jnp.exp(cum[0:1]), v, sb_ref)
        a0 = pl.multiple_of((nb - 1 - n) * (cps * c) + r0, c)
        acc_ref[pl.ds(a0, c), :] += o

    def both(j, carry):
        fwd_chunk(j)
        bwd_chunk(j)
        return carry

    lax.fori_loop(0, cps, both, 0, unroll=min(cps, 4))

    @pl.when(n == nb - 1)
    def _():
        sf_out_ref[...] = sf_ref[...]
        sb_out_ref[...] = sb_ref[...]
        gain = gain_ref[...]
        rb = 256

        def gate_rows(i, carry):
            rows = pl.ds(pl.multiple_of(i * rb, rb), rb)
            o = acc_ref[rows, :]
            gt = gate_ref[rows, :].astype(F32)
            normed = jnp.concatenate(
                [_rms(o[:, h * GLA_DV:(h + 1) * GLA_DV], gain) for h in range(GLA_HEADS)], axis=1)
            out_ref[rows, :] = (normed * _silu(gt)).astype(BF16)
            return carry

        lax.fori_loop(0, acc_ref.shape[0] // rb, gate_rows, 0)


def _gla(q, k, v, gf, gb, gate, gain, s0f, s0b, consts, *, cps):
    b, l, _ = q.shape
    t = cps * CHUNK
    nb = l // t
    nmain, nbwd, lmask = consts
    fwd = lambda w: pl.BlockSpec((None, t, w), lambda i, n: (i, n, 0))
    bwd = lambda w: pl.BlockSpec((None, t, w), lambda i, n: (i, nb - 1 - n, 0))
    whole = lambda w: pl.BlockSpec((None, l, w), lambda i, n: (i, 0, 0))
    state = pl.BlockSpec((None, 2, 2 * GLA_DV, 2 * GLA_DK), lambda i, n: (i, 0, 0, 0))
    const = lambda a: pl.BlockSpec(a.shape, lambda i, n: (0,) * a.ndim)
    sshape = jax.ShapeDtypeStruct((b, 2, 2 * GLA_DV, 2 * GLA_DK), F32)
    return pl.pallas_call(
        functools.partial(_gla_kernel, cps=cps),
        grid=(b, nb),
        in_specs=[fwd(GLA_KW), fwd(GLA_KW), fwd(GLA_VW), fwd(GLA_KW), fwd(GLA_KW),
                  bwd(GLA_KW), bwd(GLA_KW), bwd(GLA_VW), bwd(GLA_KW),
                  whole(GLA_VW), const(gain), state, state, const(nmain), const(nbwd), const(lmask)],
        out_specs=[whole(GLA_VW), state, state],
        out_shape=[jax.ShapeDtypeStruct((b, l, GLA_VW), BF16), sshape, sshape],
        scratch_shapes=[pltpu.VMEM((l, GLA_VW), F32),
                        pltpu.VMEM((2, 2 * GLA_DV, 2 * GLA_DK), F32),
                        pltpu.VMEM((2, 2 * GLA_DV, 2 * GLA_DK), F32)],
        compiler_params=_cparams(("parallel", "arbitrary")),
        name="gla_scan",
    )(q, k, v, gf, gb, q, k, v, gb, gate, gain, s0f, s0b, nmain, nbwd, lmask)


MXU_TILE = 256


def _ff_chunks(ff, tf):
    return [(f0, min(f0 + tf, ff)) for f0 in range(0, ff, tf)]


def _swiglu(xs, w1_ref, w3_ref, w2_ref, tf):
    k = xs[0].shape[1]
    acc = None
    for f0, f1 in _ff_chunks(w1_ref.shape[1], tf):
        a = sum(jnp.dot(x, w1_ref[r * k:(r + 1) * k, f0:f1], preferred_element_type=F32) for r, x in enumerate(xs))
        b = sum(jnp.dot(x, w3_ref[r * k:(r + 1) * k, f0:f1], preferred_element_type=F32) for r, x in enumerate(xs))
        y = jnp.dot((_silu(a) * b).astype(BF16), w2_ref[f0:f1, :], preferred_element_type=F32)
        acc = y if acc is None else acc + y
    return acc


F8 = jnp.float8_e4m3fn
F8_PEAK = 256.0
TINY = 1e-30
LOG2E = 1.4426950408889634


def _quant_kernel(w_ref, q_ref, inv_ref, *, rb):
    nblk = w_ref.shape[0] // rb

    def peak(i, m):
        blk = jnp.abs(w_ref[pl.ds(pl.multiple_of(i * rb, rb), rb), :])
        return jnp.maximum(m, jnp.max(blk, axis=0, keepdims=True))

    colmax = lax.fori_loop(0, nblk, peak, jnp.zeros((1, w_ref.shape[1]), F32))
    amax = jnp.maximum(jnp.max(colmax, axis=1, keepdims=True), TINY)
    scale = F8_PEAK / amax

    def cast(i, carry):
        rows = pl.ds(pl.multiple_of(i * rb, rb), rb)
        q_ref[rows, :] = (w_ref[rows, :] * scale).astype(F8)
        return carry

    lax.fori_loop(0, nblk, cast, 0)
    inv_ref[...] = jnp.broadcast_to(amax * (1.0 / F8_PEAK), inv_ref.shape)


def _quantize(w):
    ne, r, c = w.shape
    return pl.pallas_call(
        functools.partial(_quant_kernel, rb=128),
        grid=(ne,),
        in_specs=[pl.BlockSpec((None, r, c), lambda e: (e, 0, 0))],
        out_specs=[pl.BlockSpec((None, r, c), lambda e: (e, 0, 0)), pl.BlockSpec((None, 1, LANES), lambda e: (e, 0, 0))],
        out_shape=[jax.ShapeDtypeStruct((ne, r, c), F8), jax.ShapeDtypeStruct((ne, 1, LANES), F32)],
        compiler_params=_cparams(("parallel",)),
        name="quantize_weights",
    )(w)


def _row_scale(parts):
    amax = functools.reduce(jnp.maximum, [jnp.max(jnp.abs(p), axis=1, keepdims=True) for p in parts])
    amax = jnp.maximum(amax, TINY)
    return F8_PEAK / amax, amax * (1.0 / F8_PEAK)


def _swiglu_f8(xs, w1_ref, w3_ref, w2_ref, inv1, inv3, inv2, tf):
    k = xs[0].shape[1]
    sx, ix = _row_scale(xs)
    x8 = [(x * sx).astype(F8) for x in xs]
    acc = None
    for f0, f1 in _ff_chunks(w1_ref.shape[1], tf):
        a = sum(jnp.dot(x, w1_ref[r * k:(r + 1) * k, f0:f1], preferred_element_type=F32) for r, x in enumerate(x8))
        b = sum(jnp.dot(x, w3_ref[r * k:(r + 1) * k, f0:f1], preferred_element_type=F32) for r, x in enumerate(x8))
        c1 = ix * inv1
        m = a * b / (1.0 + jnp.exp2(a * (c1 * -LOG2E)))
        sm, im = _row_scale([m])
        y = (jnp.dot((m * sm).astype(F8), w2_ref[f0:f1, :], preferred_element_type=F32)
             * (im * c1 * (ix * inv3) * inv2))
        acc = y if acc is None else acc + y
    return acc


def _token_mix_residual(x_ref, gla_ref, yc_ref, gt_ref, gp_ref, wo_ref, rows=slice(None)):
    y = (jnp.dot(gla_ref[rows, :], wo_ref[:GLA_VW, :], preferred_element_type=F32)
         + jnp.dot(yc_ref[rows, :], wo_ref[GLA_VW:, :], preferred_element_type=F32))
    return x_ref[rows, :] + gt_ref[...] * _rms(y, gp_ref[...])


def _mix_specs(mod_row, gpm, wo, tm):
    d = D_MODEL
    return [pl.BlockSpec((tm, d), lambda i: (i, 0)), pl.BlockSpec((tm, GLA_VW), lambda i: (i, 0)),
            pl.BlockSpec((tm, CONV_CH), lambda i: (i, 0)),
            pl.BlockSpec((None, 1, d), lambda i: (mod_row(i), 0, 2)),
            pl.BlockSpec(gpm.shape, lambda i: (0, 0)),
            pl.BlockSpec(wo.shape, lambda i: (0, 0), pipeline_mode=pl.Buffered(1))]


def _dense_kernel(x_ref, gla_ref, yc_ref, gt1_ref, gpm_ref, wo_ref, sh_ref, sc_ref, gt_ref, gpre_ref, gpost_ref,
                  w1_ref, w3_ref, w2_ref, o_ref, *, tf, sub):
    for r0 in range(0, x_ref.shape[0], sub):
        rows = slice(r0, r0 + sub)
        x1 = _token_mix_residual(x_ref, gla_ref, yc_ref, gt1_ref, gpm_ref, wo_ref, rows)
        h = _rms(x1, gpre_ref[...]) * (1.0 + sc_ref[...]) + sh_ref[...]
        y = _swiglu([h.astype(BF16)], w1_ref, w3_ref, w2_ref, tf)
        o_ref[rows, :] = x1 + gt_ref[...] * _rms(y, gpost_ref[...])


def _dense_mix(x2, gla, yc, mod3, mod_row, gpm, wo, gpre, gpost, w1, w3, w2, *, tm, tf):
    n, d = x2.shape
    row = lambda c: pl.BlockSpec((None, 1, d), lambda i: (mod_row(i), 0, c))
    const = lambda a: pl.BlockSpec(a.shape, lambda i: (0,) * a.ndim)
    held = lambda a: pl.BlockSpec(a.shape, lambda i: (0,) * a.ndim, pipeline_mode=pl.Buffered(1))
    return pl.pallas_call(
        functools.partial(_dense_kernel, tf=tf, sub=min(tm, 512)),
        grid=(n // tm,),
        in_specs=_mix_specs(mod_row, gpm, wo, tm) + [row(3), row(4), row(5), const(gpre), const(gpost),
                                                     held(w1), held(w3), held(w2)],
        out_specs=pl.BlockSpec((tm, d), lambda i: (i, 0)),
        out_shape=jax.ShapeDtypeStruct((n, d), F32),
        compiler_params=_cparams(("parallel",)),
        name="dense_mix",
    )(x2, gla, yc, mod3, gpm, wo, mod3, mod3, mod3, gpre, gpost, w1, w3, w2)


HALF = D_MODEL // 2
HI_MASK = 0xFFFF0000


def _pack_rows(v):
    bits = pltpu.bitcast(v.astype(BF16).astype(F32), jnp.uint32)
    return (bits[:, :HALF] >> 16) | (bits[:, HALF:] & jnp.uint32(HI_MASK))


def _unpack_rows(w):
    lo = pltpu.bitcast(w << 16, F32)
    hi = pltpu.bitcast(w & jnp.uint32(HI_MASK), F32)
    return jnp.concatenate([lo, hi], axis=1)


def _route_kernel(x_ref, gla_ref, yc_ref, gt1_ref, gpm_ref, wo_ref, sh_ref, sc_ref, gpre_ref, wr_ref, tri_ref,
                  x1_ref, hp_ref, info_ref, cnt_ref, carry_ref):
    i = pl.program_id(0)

    @pl.when(i == 0)
    def _():
        carry_ref[...] = jnp.zeros_like(carry_ref)

    sub = tri_ref.shape[0]
    carry = carry_ref[...]
    for r0 in range(0, x_ref.shape[0], sub):
        rows = slice(r0, r0 + sub)
        x1 = _token_mix_residual(x_ref, gla_ref, yc_ref, gt1_ref, gpm_ref, wo_ref, rows)
        x1_ref[rows, :] = x1
        h = _rms(x1, gpre_ref[...]) * (1.0 + sc_ref[...]) + sh_ref[...]
        hp_ref[rows, :] = _pack_rows(h)
        logits = _dot_split(h, wr_ref[...])
        lane = lax.broadcasted_iota(jnp.int32, logits.shape, 1).astype(F32)
        logits = jnp.where(lane < N_EXPERTS, logits, -jnp.inf)
        m1 = jnp.max(logits, axis=-1, keepdims=True)
        i1 = jnp.min(jnp.where(logits == m1, lane, float(LANES)), axis=-1, keepdims=True)
        rest = jnp.where(lane == i1, -jnp.inf, logits)
        m2 = jnp.max(rest, axis=-1, keepdims=True)
        i2 = jnp.min(jnp.where(rest == m2, lane, float(LANES)), axis=-1, keepdims=True)
        e2 = jnp.exp(m2 - m1)
        den = 1.0 + e2
        pick = jnp.where((lane == i1) | (lane == i2), 1.0, 0.0)
        rank = jnp.dot(tri_ref[...], pick.astype(BF16), preferred_element_type=F32) + carry
        carry = carry + jnp.sum(pick, axis=0, keepdims=True)
        r1 = jnp.sum(jnp.where(lane == i1, rank, 0.0), axis=-1, keepdims=True)
        r2 = jnp.sum(jnp.where(lane == i2, rank, 0.0), axis=-1, keepdims=True)
        info = jnp.zeros_like(logits)
        for col, val in enumerate((i1, i2, 1.0 / den, e2 / den, r1, r2)):
            info = jnp.where(lane == col, val, info)
        info_ref[rows, :] = info
    carry_ref[...] = carry
    cnt_ref[...] = carry


def _route(x2, gla, yc, mod3, mod_row, gpm, wo, gpre, w_r, *, tm):
    n, d = x2.shape
    row = lambda c: pl.BlockSpec((None, 1, d), lambda i: (mod_row(i), 0, c))
    const = lambda a: pl.BlockSpec(a.shape, lambda i: (0,) * a.ndim)
    sub = min(tm, 256)
    tri = jnp.asarray(np.tril(np.ones((sub, sub), np.float32), -1), BF16)
    return pl.pallas_call(
        _route_kernel,
        grid=(n // tm,),
        in_specs=_mix_specs(mod_row, gpm, wo, tm) + [row(3), row(4), const(gpre), const(w_r), const(tri)],
        out_specs=[pl.BlockSpec((tm, d), lambda i: (i, 0)), pl.BlockSpec((tm, HALF), lambda i: (i, 0)),
                   pl.BlockSpec((tm, LANES), lambda i: (i, 0)), pl.BlockSpec((1, LANES), lambda i: (0, 0))],
        out_shape=[jax.ShapeDtypeStruct((n, d), F32), jax.ShapeDtypeStruct((n, HALF), jnp.uint32),
                   jax.ShapeDtypeStruct((n, LANES), F32), jax.ShapeDtypeStruct((1, LANES), F32)],
        scratch_shapes=[pltpu.VMEM((1, LANES), F32)],
        compiler_params=_cparams(("arbitrary",)),
        name="moe_route",
    )(x2, gla, yc, mod3, gpm, wo, mod3, mod3, gpre, w_r, tri)


def _row_copy(src, s, dst, t, sem):
    return pltpu.make_async_copy(src.at[pl.ds(s, 1), :], dst.at[pl.ds(t, 1), :], sem)


DMA_UNROLL = 8


def _dispatch_kernel(pos_ref, hp_ref, xs_in_ref, xs_ref, sem, *, tm):
    del xs_in_ref

    def start(t, carry):
        _row_copy(hp_ref, t, xs_ref, pos_ref[0, 2 * t], sem).start(priority=0)
        _row_copy(hp_ref, t, xs_ref, pos_ref[0, 2 * t + 1], sem).start(priority=1)
        return carry

    def wait(t, carry):
        _row_copy(hp_ref, 0, xs_ref, 0, sem).wait()
        return carry

    lax.fori_loop(0, tm, start, 0, unroll=DMA_UNROLL)
    lax.fori_loop(0, 2 * tm, wait, 0, unroll=DMA_UNROLL)


def _dispatch(pos3, hp, n_sorted, *, tm):
    n = hp.shape[0]
    xs0 = jnp.zeros((n_sorted, HALF), jnp.uint32)
    return pl.pallas_call(
        functools.partial(_dispatch_kernel, tm=tm),
        grid=(n // tm,),
        in_specs=[pl.BlockSpec((None, 1, 2 * tm), lambda i: (i, 0, 0), memory_space=pltpu.SMEM),
                  pl.BlockSpec((tm, HALF), lambda i: (i, 0)), pl.BlockSpec(memory_space=pl.ANY)],
        out_specs=pl.BlockSpec(memory_space=pl.ANY),
        out_shape=jax.ShapeDtypeStruct((n_sorted, HALF), jnp.uint32),
        scratch_shapes=[pltpu.SemaphoreType.DMA],
        input_output_aliases={2: 0},
        compiler_params=pltpu.CompilerParams(dimension_semantics=("arbitrary",), has_side_effects=True),
        name="moe_dispatch",
    )(pos3, hp, xs0)


def _expert_kernel(te_ref, nu_ref, xs_ref, w1_ref, w3_ref, w2_ref, i1_ref, i3_ref, i2_ref, ys_ref, *, tf):
    del te_ref

    @pl.when(pl.program_id(0) < nu_ref[0])
    def _():
        xw = xs_ref[...]
        lo = pltpu.bitcast(xw << 16, F32)
        hi = pltpu.bitcast(xw & jnp.uint32(HI_MASK), F32)
        inv = [r[:, :1] for r in (i1_ref[...], i3_ref[...], i2_ref[...])]
        ys_ref[...] = _pack_rows(_swiglu_f8([lo, hi], w1_ref, w3_ref, w2_ref, *inv, tf))


def _experts(tile_expert, n_used, xs, w1, w3, w2, *, tmg, tf):
    n_sorted = xs.shape[0]
    _, d, ff = w1.shape
    (q1, i1), (q3, i3), (q2, i2) = _quantize(w1), _quantize(w3), _quantize(w2)
    blk = lambda j, te, nu: (jnp.minimum(j, nu[0] - 1), 0)
    held = lambda r, c: pl.BlockSpec((None, r, c), lambda j, te, nu: (te[j], 0, 0), pipeline_mode=pl.Buffered(1))
    grid_spec = pltpu.PrefetchScalarGridSpec(
        num_scalar_prefetch=2,
        grid=(n_sorted // tmg,),
        in_specs=[pl.BlockSpec((tmg, HALF), blk), held(d, ff), held(d, ff), held(ff, d),
                  held(1, LANES), held(1, LANES), held(1, LANES)],
        out_specs=pl.BlockSpec((tmg, HALF), blk),
    )
    return pl.pallas_call(
        functools.partial(_expert_kernel, tf=tf),
        grid_spec=grid_spec,
        out_shape=jax.ShapeDtypeStruct((n_sorted, HALF), jnp.uint32),
        compiler_params=_cparams(("arbitrary",)),
        name="moe_experts",
    )(tile_expert, n_used, xs, q1, q3, q2, i1, i3, i2)


def _combine_kernel(pos_ref, x_ref, gt_ref, gpost_ref, info_ref, ys_ref, o_ref, buf_ref, sems, *, tm):
    i = pl.program_id(0)
    slot = i & 1

    @pl.when(i < pl.num_programs(0) - 1)
    def _():
        def start(t, carry):
            _row_copy(ys_ref, pos_ref[0, 2 * t], buf_ref.at[slot, 0], t, sems.at[slot]).start(priority=0)
            _row_copy(ys_ref, pos_ref[0, 2 * t + 1], buf_ref.at[slot, 1], t, sems.at[slot]).start(priority=1)
            return carry

        lax.fori_loop(0, tm, start, 0, unroll=DMA_UNROLL)

    @pl.when(i > 0)
    def _():
        prev = 1 - slot

        def wait(t, carry):
            _row_copy(ys_ref, 0, buf_ref.at[prev, 0], 0, sems.at[prev]).wait()
            return carry

        lax.fori_loop(0, 2 * tm, wait, 0, unroll=DMA_UNROLL)
        info = info_ref[...]
        lane = lax.broadcasted_iota(jnp.int32, info.shape, 1)
        wt1 = jnp.sum(jnp.where(lane == 2, info, 0.0), axis=-1, keepdims=True)
        wt2 = jnp.sum(jnp.where(lane == 3, info, 0.0), axis=-1, keepdims=True)
        y = wt1 * _unpack_rows(buf_ref[prev, 0]) + wt2 * _unpack_rows(buf_ref[prev, 1])
        o_ref[...] = x_ref[...] + gt_ref[...] * _rms(y, gpost_ref[...])


def _combine(pos3, x2, mod3, mod_row, gpost, info, ys, *, tm):
    n, d = x2.shape
    nt = n // tm
    done = lambda i: jnp.maximum(i - 1, 0)
    tile = pl.BlockSpec((tm, d), lambda i: (done(i), 0))
    return pl.pallas_call(
        functools.partial(_combine_kernel, tm=tm),
        grid=(nt + 1,),
        in_specs=[pl.BlockSpec((None, 1, 2 * tm), lambda i: (jnp.minimum(i, nt - 1), 0, 0),
                               memory_space=pltpu.SMEM),
                  tile, pl.BlockSpec((None, 1, d), lambda i: (mod_row(done(i)), 0, 5)),
                  pl.BlockSpec(gpost.shape, lambda i: (0, 0)),
                  pl.BlockSpec((tm, LANES), lambda i: (done(i), 0)), pl.BlockSpec(memory_space=pl.ANY)],
        out_specs=tile,
        out_shape=jax.ShapeDtypeStruct((n, d), F32),
        scratch_shapes=[pltpu.VMEM((2, 2, tm, HALF), jnp.uint32), pltpu.SemaphoreType.DMA((2,))],
        compiler_params=_cparams(("arbitrary",)),
        name="moe_combine",
    )(pos3, x2, mod3, gpost, info, ys)


def _moe_mix(x2, gla, yc, mod3, mod_row, gpm, wo, gpre, gpost, w_r, w1, w3, w2, *, tm, tmg, tf):
    n = x2.shape[0]
    x2, hp, info, cnt = _route(x2, gla, yc, mod3, mod_row, gpm, wo, gpre, w_r, tm=tm)
    counts = cnt[0, :N_EXPERTS].astype(jnp.int32)
    padded = ((counts + tmg - 1) // tmg) * tmg
    ends = jnp.cumsum(padded)
    starts = ends - padded
    n_tiles = (2 * n) // tmg + N_EXPERTS
    picks = info[:, 0:2].astype(jnp.int32)
    pos = starts[picks] + info[:, 4:6].astype(jnp.int32)
    pos3 = pos.reshape(n // tm, 1, 2 * tm)
    tile_expert = jnp.sum(jnp.arange(n_tiles, dtype=jnp.int32)[:, None] * tmg >= ends[None, :], axis=1)
    n_used = (ends[-1] // tmg).astype(jnp.int32).reshape(1)
    tile_expert = jnp.minimum(tile_expert, tile_expert[jnp.maximum(n_used[0] - 1, 0)]).astype(jnp.int32)
    xs = _dispatch(pos3, hp, n_tiles * tmg, tm=tm)
    ys = _experts(tile_expert, n_used, xs, w1, w3, w2, tmg=tmg, tf=tf)
    return _combine(pos3, x2, mod3, mod_row, gpost, info, ys, tm=tm)


def kernel(x, c, ctx, c_ctx, w_mod, b_mod, g_mix_pre, g_mix_post, w_in, w_decay, b_decay, gla_norm,
           conv_w, w_out, g_ffn_pre, g_ffn_post, w1, w3, w2, w_router, e_w1, e_w3, e_w2):
    bsz, seq, d = x.shape
    ctx_len = ctx.shape[1]
    depth = w_mod.shape[0]
    assert d == D_MODEL and ctx_len % CHUNK == 0 and bsz + 1 <= MOD_ROWS

    c_all = jnp.zeros((MOD_ROWS, d), F32).at[:bsz].set(c).at[bsz].set(c_ctx)
    mod = _modulation(c_all, w_mod, b_mod)
    consts = _gla_constants()

    tm_x = 512
    tm_p = 1024
    tm_c = ctx_len
    assert seq % tm_p == 0 and seq % tm_x == 0 and ctx_len & (ctx_len - 1) == 0
    x_row = lambda i: i // (seq // tm_x)
    c_row = lambda i: bsz
    x2 = x.reshape(bsz * seq, d)
    xc2 = ctx.reshape(bsz * ctx_len, d)
    zero_state = jnp.zeros((bsz, 2, 2 * GLA_DV, 2 * GLA_DK), F32)
    row2 = lambda a: a.reshape(1, -1)

    o_q, o_k, o_v = 0, GLA_KW, 2 * GLA_KW
    o_g = o_v + GLA_VW
    o_a = o_g + GLA_VW
    o_c = o_a + 2 * DECAY_RANK

    for i in range(depth):
        last = i == depth - 1
        mod3 = mod[i].reshape(MOD_ROWS, 1, 6 * d)
        wi = w_in[i]
        wm = wi[:, :o_a].astype(BF16)
        wa = jnp.zeros((d, LANES), F32).at[:, :2 * DECAY_RANK].set(wi[:, o_a:o_c]).astype(BF16)
        wc = wi[:, o_c:].astype(BF16)
        wd = jnp.zeros((LANES, 2 * GLA_KW), F32)
        wd = wd.at[:DECAY_RANK, :GLA_KW].set(w_decay[i, 0]).at[DECAY_RANK:2 * DECAY_RANK, GLA_KW:].set(w_decay[i, 1])
        bd = b_decay[i].reshape(1, 2 * GLA_KW)
        wo = w_out[i].astype(BF16)
        gain = row2(gla_norm[i])

        def mix(tokens, mod_row, tm, seg, nseq, slen, s0f, s0b, cps):
            q, k, v, gate, gf, gb, yc = _project(tokens, mod3, mod_row, row2(g_mix_pre[i]), wm, wa, wc, wd, bd,
                                                 conv_w[i], tm=tm, seg=seg)
            r3 = lambda a: a.reshape(nseq, slen, a.shape[-1])
            o, sf, sb = _gla(r3(q), r3(k), r3(v), r3(gf), r3(gb), r3(gate), gain, s0f, s0b, consts, cps=cps)
            return o.reshape(nseq * slen, GLA_VW), yc, sf, sb

        o_c_, yc_c, s_f, s_b = mix(xc2, c_row, tm_c, ctx_len, bsz, ctx_len, zero_state, zero_state,
                                   ctx_len // CHUNK)
        o_x, yc_x, _, _ = mix(x2, lambda t: t // (seq // tm_p), tm_p, GRID_W, bsz, seq, s_f, s_b, 8)

        j = i // 2
        common = dict(gpm=row2(g_mix_post[i]), wo=wo, gpre=row2(g_ffn_pre[i]), gpost=row2(g_ffn_post[i]))
        if i % 2 == 0:
            ffn = functools.partial(_dense_mix, w1=w1[j].astype(BF16), w3=w3[j].astype(BF16),
                                    w2=w2[j].astype(BF16), tf=4 * MXU_TILE, **common)
        else:
            w_r = jnp.zeros((d, LANES), F32).at[:, :N_EXPERTS].set(w_router[j])
            ffn = functools.partial(_moe_mix, w_r=w_r, w1=e_w1[j], w3=e_w3[j], w2=e_w2[j], tmg=512,
                                    tf=7 * MXU_TILE, **common)
        x2 = ffn(x2, o_x, yc_x, mod3, x_row, tm=tm_x)
        if not last:
            xc2 = ffn(xc2, o_c_, yc_c, mod3, c_row, tm=tm_c)
    return x2.reshape(bsz, seq, d)
```

```python
import functools

import numpy as np
import jax
import jax.numpy as jnp
from jax import lax
from jax.experimental import pallas as pl
from jax.experimental.pallas import tpu as pltpu

F32 = jnp.float32
BF16 = jnp.bfloat16
HIGHEST = lax.Precision.HIGHEST

D_MODEL = 1024
GLA_HEADS = 4
GLA_DK = 64
GLA_DV = 128
GLA_KW = GLA_HEADS * GLA_DK
GLA_VW = GLA_HEADS * GLA_DV
DECAY_RANK = 16
GATE_NORMALIZER = 16.0
CHUNK = 64
CONV_CH = D_MODEL - GLA_VW
GRID_W = 64
N_EXPERTS = 8
EPS = 1e-6
LANES = 128
N_LEVELS = 6
MM_LEVELS = N_LEVELS - 1
MOD_ROWS = 24
VMEM_LIMIT = 52 * 1024 * 1024


def _cparams(sem):
    return pltpu.CompilerParams(dimension_semantics=sem, vmem_limit_bytes=VMEM_LIMIT)


def _rms(x, gain):
    return x * lax.rsqrt(jnp.mean(x * x, axis=-1, keepdims=True) + EPS) * gain


def _silu(x):
    return x / (1.0 + jnp.exp(-x))


def _dot_split(a, b):
    ah = a.astype(BF16)
    al = (a - ah.astype(F32)).astype(BF16)
    bh = b.astype(BF16)
    bl = (b - bh.astype(F32)).astype(BF16)
    dot = functools.partial(jnp.dot, preferred_element_type=F32)
    return dot(ah, bh) + (dot(ah, bl) + dot(al, bh))


def _mod_kernel(c_ref, w_ref, b_ref, o_ref):
    s = _silu(c_ref[...])
    o_ref[...] = jnp.dot(s, w_ref[...], precision=HIGHEST, preferred_element_type=F32) + b_ref[...]


def _modulation(c_all, w_mod, b_mod):
    depth, d, n = w_mod.shape
    tn = 1536
    return pl.pallas_call(
        _mod_kernel,
        grid=(depth, n // tn),
        in_specs=[
            pl.BlockSpec((MOD_ROWS, d), lambda i, j: (0, 0)),
            pl.BlockSpec((None, d, tn), lambda i, j: (i, 0, j)),
            pl.BlockSpec((None, 1, tn), lambda i, j: (i, 0, j)),
        ],
        out_specs=pl.BlockSpec((None, MOD_ROWS, tn), lambda i, j: (i, 0, j)),
        out_shape=jax.ShapeDtypeStruct((depth, MOD_ROWS, n), F32),
        compiler_params=_cparams(("parallel", "parallel")),
        name="modulation",
    )(c_all, w_mod, b_mod.reshape(depth, 1, n))


def _proj_kernel(x_ref, sh_ref, sc_ref, g_ref, wm_ref, wa_ref, wc_ref, wd_ref, bd_ref, cw_ref,
                 q_ref, k_ref, v_ref, gate_ref, gf_ref, gb_ref, yc_ref, *, seg, sub):
    for r0 in range(0, x_ref.shape[0], sub):
        rows = slice(r0, r0 + sub)
        h = _rms(x_ref[rows, :], g_ref[...]) * (1.0 + sc_ref[...]) + sh_ref[...]
        hb = h.astype(BF16)
        p = jnp.dot(hb, wm_ref[...], preferred_element_type=F32)
        q_ref[rows, :] = (p[:, :GLA_KW] * (GLA_DK ** -0.5)).astype(BF16)
        k_ref[rows, :] = p[:, GLA_KW:2 * GLA_KW].astype(BF16)
        v_ref[rows, :] = p[:, 2 * GLA_KW:2 * GLA_KW + GLA_VW].astype(BF16)
        gate_ref[rows, :] = p[:, 2 * GLA_KW + GLA_VW:].astype(BF16)
        pa = jnp.dot(hb, wa_ref[...], preferred_element_type=F32)
        xd = _dot_split(pa, wd_ref[...]) + bd_ref[...]
        ls = (jnp.minimum(xd, 0.0) - jnp.log1p(jnp.exp(-jnp.abs(xd)))) * (LOG2E / GATE_NORMALIZER)
        gf_ref[rows, :] = ls[:, :GLA_KW]
        gb_ref[rows, :] = ls[:, GLA_KW:]
        pc = jnp.dot(hb, wc_ref[...], preferred_element_type=F32)
        u = pc[:, CONV_CH:2 * CONV_CH] * pc[:, 2 * CONV_CH:]
        row = lax.broadcasted_iota(jnp.int32, u.shape, 0) & (seg - 1)
        u_prev = jnp.where(row == 0, 0.0, pltpu.roll(u, 1, 0))
        u_next = jnp.where(row == seg - 1, 0.0, pltpu.roll(u, sub - 1, 0))
        cw = cw_ref[...]
        yc = pc[:, :CONV_CH] * (cw[0:1] * u_prev + cw[1:2] * u + cw[2:3] * u_next)
        yc_ref[rows, :] = yc.astype(BF16)


def _project(x2, mod3, mod_row, gain, wm, wa, wc, wd, bd, cw, *, tm, seg):
    n = x2.shape[0]
    d = D_MODEL
    row = lambda c: pl.BlockSpec((None, 1, d), lambda i: (mod_row(i), 0, c))
    const = lambda a: pl.BlockSpec(a.shape, lambda i: (0,) * a.ndim, pipeline_mode=pl.Buffered(1))
    tile = lambda w: pl.BlockSpec((tm, w), lambda i: (i, 0))
    shapes = [(GLA_KW, BF16), (GLA_KW, BF16), (GLA_VW, BF16), (GLA_VW, BF16),
              (GLA_KW, F32), (GLA_KW, F32), (CONV_CH, BF16)]
    return pl.pallas_call(
        functools.partial(_proj_kernel, seg=seg, sub=min(tm, 512)),
        grid=(n // tm,),
        in_specs=[tile(d), row(0), row(1), const(gain), const(wm), const(wa), const(wc),
                  const(wd), const(bd), const(cw)],
        out_specs=[tile(w) for w, _ in shapes],
        out_shape=[jax.ShapeDtypeStruct((n, w), t) for w, t in shapes],
        compiler_params=_cparams(("parallel",)),
        name="in_proj",
    )(x2, mod3, mod3, gain, wm, wa, wc, wd, bd, cw)


def _gla_constants():
    c = CHUNK
    main = np.zeros((2 * MM_LEVELS + 2, c, 2 * c), np.float32)
    lmask = np.zeros((N_LEVELS + 1, c, c), np.float32)
    for lvl in range(N_LEVELS):
        s = c >> (lvl + 1)
        for i in range(c):
            mid = (i // (2 * s)) * 2 * s + s
            if lvl < MM_LEVELS and i >= mid:
                main[2 * lvl, i, mid:i + 1] = 1.0
                main[2 * lvl + 1, i, c + mid:c + i] = 1.0
            elif lvl < MM_LEVELS:
                main[2 * lvl, i, c + i:c + mid] = 1.0
                main[2 * lvl + 1, i, i + 1:mid] = 1.0
            for j in range(c):
                same = (j // (2 * s)) == (i // (2 * s))
                lmask[lvl, i, j] = float(same and ((i >= mid) != (j >= mid)))
    lmask[N_LEVELS] = 2.0 * np.eye(c)
    bwd = np.zeros((2, c, c), np.float32)
    for i in range(c):
        main[2 * MM_LEVELS, i, :i + 1] = 1.0
        main[2 * MM_LEVELS + 1, i, i + 1:c] = 1.0
        bwd[0, i, i:] = 1.0
        bwd[1, i, :i] = 1.0
    main = main.reshape(-1, 2 * c)
    bwd = bwd.reshape(-1, c)
    lmask = np.concatenate([lmask, lmask], axis=-1)
    return jnp.asarray(main, BF16), jnp.asarray(bwd, BF16), jnp.asarray(lmask, F32)


def _sum_dot(n_ref, g, parts):
    gb = g.astype(BF16)
    rows = n_ref.shape[0] // parts
    outs = [jnp.dot(n_ref[p * rows:(p + 1) * rows, :], gb, preferred_element_type=F32) for p in range(parts)]
    per = rows // CHUNK
    return lambda b: outs[b // per][(b % per) * CHUNK:(b % per + 1) * CHUNK]


_NT = (((1,), (1,)), ((), ()))
_TN = (((0,), (0,)), ((), ()))


def _gla_kernel(qf_ref, kf_ref, vf_ref, gff_ref, gbf_ref, qb_ref, kb_ref, vb_ref, gbb_ref,
                gate_ref, gain_ref, s0f_ref, s0b_ref, nmain_ref, nbwd_ref, lmask_ref,
                out_ref, sf_out_ref, sb_out_ref, acc_ref, sf_ref, sb_ref, *, cps):
    n = pl.program_id(1)
    nb = pl.num_programs(1)
    c = CHUNK
    pw = 2 * GLA_DK
    vw = 2 * GLA_DV

    @pl.when(n == 0)
    def _():
        acc_ref[...] = jnp.zeros_like(acc_ref)
        sf_ref[...] = s0f_ref[...]
        sb_ref[...] = s0b_ref[...]

    klane = lax.broadcasted_iota(jnp.int32, (c, GLA_KW), 1)
    k_even = (klane & (pw - 1)) < GLA_DK
    vlane = lax.broadcasted_iota(jnp.int32, (c, vw), 1)
    v_low = vlane < GLA_DV
    srow = lax.broadcasted_iota(jnp.int32, (vw, pw), 0)
    scol = lax.broadcasted_iota(jnp.int32, (vw, pw), 1)
    s_diag = (srow < GLA_DV) == (scol < GLA_DK)
    zero_b = jnp.zeros((), BF16)

    def inter(qi, ki, dec, v, s_ref):
        outs = []
        for p in range(2):
            st = s_ref[p]
            outs.append(lax.dot_general(qi[:, p * pw:(p + 1) * pw], st.astype(BF16), _NT,
                                        preferred_element_type=F32))
            upd = lax.dot_general(v[:, p * vw:(p + 1) * vw], ki[:, p * pw:(p + 1) * pw], _TN,
                                  preferred_element_type=F32)
            s_ref[p] = st * dec[:, p * pw:(p + 1) * pw] + jnp.where(s_diag, upd, 0.0)
        return jnp.concatenate(outs, axis=1)

    odd_row = (lax.broadcasted_iota(jnp.int32, (c, GLA_KW), 0) & 1) == 1

    def fwd_chunk(j):
        r0 = pl.multiple_of(j * c, c)
        rows = pl.ds(r0, c)
        qb16 = qf_ref[rows, :]
        kb16 = kf_ref[rows, :]
        v = vf_ref[rows, :]
        q = qb16.astype(F32)
        k = kb16.astype(F32)
        gf = gff_ref[rows, :]
        gb = gbf_ref[rows, :]
        e = _sum_dot(nmain_ref, jnp.concatenate([gf, gb], axis=0), 2)
        att = [jnp.zeros((c, pw), F32), jnp.zeros((c, pw), F32)]
        for lvl in range(N_LEVELS + 1):
            if lvl < MM_LEVELS:
                qt = (q * jnp.exp2(e(2 * lvl))).astype(BF16)
                kt = (k * jnp.exp2(e(2 * lvl + 1))).astype(BF16)
            elif lvl < N_LEVELS:
                qt, kt = (q * jnp.exp2(jnp.where(odd_row, gf, gb))).astype(BF16), kb16
            else:
                qt, kt = qb16, kb16
            ke = jnp.where(k_even, kt, zero_b)
            ko = jnp.where(k_even, zero_b, kt)
            m = lmask_ref[lvl]
            for p in range(2):
                kbd = jnp.concatenate([ke[:, p * pw:(p + 1) * pw], ko[:, p * pw:(p + 1) * pw]], axis=0)
                att[p] = att[p] + m * lax.dot_general(qt[:, p * pw:(p + 1) * pw], kbd, _NT,
                                                      preferred_element_type=F32)
        outs = []
        for p in range(2):
            vp = v[:, p * vw:(p + 1) * vw]
            vbd = jnp.concatenate([jnp.where(v_low, vp, zero_b), jnp.where(v_low, zero_b, vp)], axis=0)
            outs.append(jnp.dot(att[p].astype(BF16), vbd, preferred_element_type=F32))
        o = jnp.concatenate(outs, axis=1)
        cum = e(2 * MM_LEVELS)
        qi = (q * jnp.exp2(cum)).astype(BF16)
        ki = (k * jnp.exp2(e(2 * MM_LEVELS + 1))).astype(BF16)
        o = o + inter(qi, ki, jnp.exp2(cum[c - 1:c]), v, sf_ref)
        a0 = pl.multiple_of(n * (cps * c) + r0, c)
        acc_ref[pl.ds(a0, c), :] += o

    def bwd_chunk(j):
        r0 = pl.multiple_of((cps - 1 - j) * c, c)
        rows = pl.ds(r0, c)
        q = qb_ref[rows, :].astype(F32)
        k = kb_ref[rows, :].astype(F32)
        v = vb_ref[rows, :]
        e = _sum_dot(nbwd_ref, gbb_ref[rows, :], 1)
        cum = e(0)
        qi = (q * jnp.exp2(cum)).astype(BF16)
        ki = (k * jnp.exp2(e(1))).astype(BF16)
        o = inter(qi, ki, jnp.exp2(cum[0:1]), v, sb_ref)
        a0 = pl.multiple_of((nb - 1 - n) * (cps * c) + r0, c)
        acc_ref[pl.ds(a0, c), :] += o

    def both(j, carry):
        fwd_chunk(j)
        bwd_chunk(j)
        return carry

    lax.fori_loop(0, cps, both, 0, unroll=min(cps, 4))

    @pl.when(n == nb - 1)
    def _():
        sf_out_ref[...] = sf_ref[...]
        sb_out_ref[...] = sb_ref[...]
        gain = gain_ref[...]
        rb = 256

        def gate_rows(i, carry):
            rows = pl.ds(pl.multiple_of(i * rb, rb), rb)
            o = acc_ref[rows, :]
            gt = gate_ref[rows, :].astype(F32)
            normed = jnp.concatenate(
                [_rms(o[:, h * GLA_DV:(h + 1) * GLA_DV], gain) for h in range(GLA_HEADS)], axis=1)
            out_ref[rows, :] = (normed * _silu(gt)).astype(BF16)
            return carry

        lax.fori_loop(0, acc_ref.shape[0] // rb, gate_rows, 0)


def _gla(q, k, v, gf, gb, gate, gain, s0f, s0b, consts, *, cps):
    b, l, _ = q.shape
    t = cps * CHUNK
    nb = l // t
    nmain, nbwd, lmask = consts
    fwd = lambda w: pl.BlockSpec((None, t, w), lambda i, n: (i, n, 0))
    bwd = lambda w: pl.BlockSpec((None, t, w), lambda i, n: (i, nb - 1 - n, 0))
    whole = lambda w: pl.BlockSpec((None, l, w), lambda i, n: (i, 0, 0))
    state = pl.BlockSpec((None, 2, 2 * GLA_DV, 2 * GLA_DK), lambda i, n: (i, 0, 0, 0))
    const = lambda a: pl.BlockSpec(a.shape, lambda i, n: (0,) * a.ndim)
    sshape = jax.ShapeDtypeStruct((b, 2, 2 * GLA_DV, 2 * GLA_DK), F32)
    return pl.pallas_call(
        functools.partial(_gla_kernel, cps=cps),
        grid=(b, nb),
        in_specs=[fwd(GLA_KW), fwd(GLA_KW), fwd(GLA_VW), fwd(GLA_KW), fwd(GLA_KW),
                  bwd(GLA_KW), bwd(GLA_KW), bwd(GLA_VW), bwd(GLA_KW),
                  whole(GLA_VW), const(gain), state, state, const(nmain), const(nbwd), const(lmask)],
        out_specs=[whole(GLA_VW), state, state],
        out_shape=[jax.ShapeDtypeStruct((b, l, GLA_VW), BF16), sshape, sshape],
        scratch_shapes=[pltpu.VMEM((l, GLA_VW), F32),
                        pltpu.VMEM((2, 2 * GLA_DV, 2 * GLA_DK), F32),
                        pltpu.VMEM((2, 2 * GLA_DV, 2 * GLA_DK), F32)],
        compiler_params=_cparams(("parallel", "arbitrary")),
        name="gla_scan",
    )(q, k, v, gf, gb, q, k, v, gb, gate, gain, s0f, s0b, nmain, nbwd, lmask)


MXU_TILE = 256


def _ff_chunks(ff, tf):
    return [(f0, min(f0 + tf, ff)) for f0 in range(0, ff, tf)]


def _swiglu(x, w1_ref, w3_ref, w2_ref, tf):
    ms = []
    for f0, f1 in _ff_chunks(w1_ref.shape[1], tf):
        a = jnp.dot(x, w1_ref[:, f0:f1], preferred_element_type=F32)
        b = jnp.dot(x, w3_ref[:, f0:f1], preferred_element_type=F32)
        ms.append((_silu(a) * b).astype(BF16))
    return jnp.dot(jnp.concatenate(ms, axis=1), w2_ref[...], preferred_element_type=F32)


F8 = jnp.float8_e4m3fn
F8_PEAK = 256.0
TINY = 1e-30
LOG2E = 1.4426950408889634


def _quant_kernel(w_ref, q_ref, inv_ref, *, rb):
    nblk = w_ref.shape[0] // rb

    def peak(i, m):
        blk = jnp.abs(w_ref[pl.ds(pl.multiple_of(i * rb, rb), rb), :])
        return jnp.maximum(m, jnp.max(blk, axis=0, keepdims=True))

    colmax = lax.fori_loop(0, nblk, peak, jnp.zeros((1, w_ref.shape[1]), F32))
    amax = jnp.maximum(jnp.max(colmax, axis=1, keepdims=True), TINY)
    scale = F8_PEAK / amax

    def cast(i, carry):
        rows = pl.ds(pl.multiple_of(i * rb, rb), rb)
        q_ref[rows, :] = (w_ref[rows, :] * scale).astype(F8)
        return carry

    lax.fori_loop(0, nblk, cast, 0)
    inv_ref[...] = jnp.broadcast_to(amax * (1.0 / F8_PEAK), inv_ref.shape)


def _quantize(w):
    ne, r, c = w.shape
    return pl.pallas_call(
        functools.partial(_quant_kernel, rb=128),
        grid=(ne,),
        in_specs=[pl.BlockSpec((None, r, c), lambda e: (e, 0, 0))],
        out_specs=[pl.BlockSpec((None, r, c), lambda e: (e, 0, 0)), pl.BlockSpec((None, 1, LANES), lambda e: (e, 0, 0))],
        out_shape=[jax.ShapeDtypeStruct((ne, r, c), F8), jax.ShapeDtypeStruct((ne, 1, LANES), F32)],
        compiler_params=_cparams(("parallel",)),
        name="quantize_weights",
    )(w)


def _row_scale(parts):
    amax = functools.reduce(jnp.maximum, [jnp.max(jnp.abs(p), axis=1, keepdims=True) for p in parts])
    amax = jnp.maximum(amax, TINY)
    return F8_PEAK / amax, amax * (1.0 / F8_PEAK)


def _swiglu_f8(xs, w1_ref, w3_ref, w2_ref, inv1, inv3, inv2, tf):
    k = xs[0].shape[1]
    sx, ix = _row_scale(xs)
    x8 = [(x * sx).astype(F8) for x in xs]
    c1 = ix * inv1
    ms = []
    for f0, f1 in _ff_chunks(w1_ref.shape[1], tf):
        a = sum(jnp.dot(x, w1_ref[r * k:(r + 1) * k, f0:f1], preferred_element_type=F32) for r, x in enumerate(x8))
        b = sum(jnp.dot(x, w3_ref[r * k:(r + 1) * k, f0:f1], preferred_element_type=F32) for r, x in enumerate(x8))
        ms.append(a * b / (1.0 + jnp.exp2(a * (c1 * -LOG2E))))
    sm, im = _row_scale(ms)
    m8 = jnp.concatenate([(m * sm).astype(F8) for m in ms], axis=1)
    return jnp.dot(m8, w2_ref[...], preferred_element_type=F32) * (im * c1 * (ix * inv3) * inv2)


def _token_mix_residual(x_ref, gla_ref, yc_ref, gt_ref, gp_ref, wo_ref, rows=slice(None)):
    y = (jnp.dot(gla_ref[rows, :], wo_ref[:GLA_VW, :], preferred_element_type=F32)
         + jnp.dot(yc_ref[rows, :], wo_ref[GLA_VW:, :], preferred_element_type=F32))
    return x_ref[rows, :] + gt_ref[...] * _rms(y, gp_ref[...])


def _mix_specs(mod_row, gpm, wo, tm):
    d = D_MODEL
    return [pl.BlockSpec((tm, d), lambda i: (i, 0)), pl.BlockSpec((tm, GLA_VW), lambda i: (i, 0)),
            pl.BlockSpec((tm, CONV_CH), lambda i: (i, 0)),
            pl.BlockSpec((None, 1, d), lambda i: (mod_row(i), 0, 2)),
            pl.BlockSpec(gpm.shape, lambda i: (0, 0)),
            pl.BlockSpec(wo.shape, lambda i: (0, 0), pipeline_mode=pl.Buffered(1))]


def _dense_kernel(x_ref, gla_ref, yc_ref, gt1_ref, gpm_ref, wo_ref, sh_ref, sc_ref, gt_ref, gpre_ref, gpost_ref,
                  w1_ref, w3_ref, w2_ref, o_ref, *, tf, sub):
    for r0 in range(0, x_ref.shape[0], sub):
        rows = slice(r0, r0 + sub)
        x1 = _token_mix_residual(x_ref, gla_ref, yc_ref, gt1_ref, gpm_ref, wo_ref, rows)
        h = _rms(x1, gpre_ref[...]) * (1.0 + sc_ref[...]) + sh_ref[...]
        y = _swiglu(h.astype(BF16), w1_ref, w3_ref, w2_ref, tf)
        o_ref[rows, :] = x1 + gt_ref[...] * _rms(y, gpost_ref[...])


def _dense_mix(x2, gla, yc, mod3, mod_row, gpm, wo, gpre, gpost, w1, w3, w2, *, tm, tf):
    n, d = x2.shape
    row = lambda c: pl.BlockSpec((None, 1, d), lambda i: (mod_row(i), 0, c))
    const = lambda a: pl.BlockSpec(a.shape, lambda i: (0,) * a.ndim)
    held = lambda a: pl.BlockSpec(a.shape, lambda i: (0,) * a.ndim, pipeline_mode=pl.Buffered(1))
    return pl.pallas_call(
        functools.partial(_dense_kernel, tf=tf, sub=min(tm, 512)),
        grid=(n // tm,),
        in_specs=_mix_specs(mod_row, gpm, wo, tm) + [row(3), row(4), row(5), const(gpre), const(gpost),
                                                     held(w1), held(w3), held(w2)],
        out_specs=pl.BlockSpec((tm, d), lambda i: (i, 0)),
        out_shape=jax.ShapeDtypeStruct((n, d), F32),
        compiler_params=_cparams(("parallel",)),
        name="dense_mix",
    )(x2, gla, yc, mod3, gpm, wo, mod3, mod3, mod3, gpre, gpost, w1, w3, w2)


HALF = D_MODEL // 2
HI_MASK = 0xFFFF0000


def _pack_rows(v):
    bits = pltpu.bitcast(v.astype(BF16).astype(F32), jnp.uint32)
    return (bits[:, :HALF] >> 16) | (bits[:, HALF:] & jnp.uint32(HI_MASK))


def _unpack_rows(w):
    lo = pltpu.bitcast(w << 16, F32)
    hi = pltpu.bitcast(w & jnp.uint32(HI_MASK), F32)
    return jnp.concatenate([lo, hi], axis=1)


def _route_kernel(x_ref, gla_ref, yc_ref, gt1_ref, gpm_ref, wo_ref, sh_ref, sc_ref, gpre_ref, wr_ref, tri_ref,
                  x1_ref, hp_ref, info_ref, cnt_ref, carry_ref):
    i = pl.program_id(0)

    @pl.when(i == 0)
    def _():
        carry_ref[...] = jnp.zeros_like(carry_ref)

    sub = tri_ref.shape[0]
    carry = carry_ref[...]
    for r0 in range(0, x_ref.shape[0], sub):
        rows = slice(r0, r0 + sub)
        x1 = _token_mix_residual(x_ref, gla_ref, yc_ref, gt1_ref, gpm_ref, wo_ref, rows)
        x1_ref[rows, :] = x1
        h = _rms(x1, gpre_ref[...]) * (1.0 + sc_ref[...]) + sh_ref[...]
        hp_ref[rows, :] = _pack_rows(h)
        logits = _dot_split(h, wr_ref[...])
        lane = lax.broadcasted_iota(jnp.int32, logits.shape, 1).astype(F32)
        logits = jnp.where(lane < N_EXPERTS, logits, -jnp.inf)
        m1 = jnp.max(logits, axis=-1, keepdims=True)
        i1 = jnp.min(jnp.where(logits == m1, lane, float(LANES)), axis=-1, keepdims=True)
        rest = jnp.where(lane == i1, -jnp.inf, logits)
        m2 = jnp.max(rest, axis=-1, keepdims=True)
        i2 = jnp.min(jnp.where(rest == m2, lane, float(LANES)), axis=-1, keepdims=True)
        e2 = jnp.exp(m2 - m1)
        den = 1.0 + e2
        pick = jnp.where((lane == i1) | (lane == i2), 1.0, 0.0)
        rank = jnp.dot(tri_ref[...], pick.astype(BF16), preferred_element_type=F32) + carry
        carry = carry + jnp.sum(pick, axis=0, keepdims=True)
        r1 = jnp.sum(jnp.where(lane == i1, rank, 0.0), axis=-1, keepdims=True)
        r2 = jnp.sum(jnp.where(lane == i2, rank, 0.0), axis=-1, keepdims=True)
        info = jnp.zeros_like(logits)
        for col, val in enumerate((i1, i2, 1.0 / den, e2 / den, r1, r2)):
            info = jnp.where(lane == col, val, info)
        info_ref[rows, :] = info
    carry_ref[...] = carry
    cnt_ref[...] = carry


def _route(x2, gla, yc, mod3, mod_row, gpm, wo, gpre, w_r, *, tm):
    n, d = x2.shape
    row = lambda c: pl.BlockSpec((None, 1, d), lambda i: (mod_row(i), 0, c))
    const = lambda a: pl.BlockSpec(a.shape, lambda i: (0,) * a.ndim)
    sub = min(tm, 256)
    tri = jnp.asarray(np.tril(np.ones((sub, sub), np.float32), -1), BF16)
    return pl.pallas_call(
        _route_kernel,
        grid=(n // tm,),
        in_specs=_mix_specs(mod_row, gpm, wo, tm) + [row(3), row(4), const(gpre), const(w_r), const(tri)],
        out_specs=[pl.BlockSpec((tm, d), lambda i: (i, 0)), pl.BlockSpec((tm, HALF), lambda i: (i, 0)),
                   pl.BlockSpec((tm, LANES), lambda i: (i, 0)), pl.BlockSpec((1, LANES), lambda i: (0, 0))],
        out_shape=[jax.ShapeDtypeStruct((n, d), F32), jax.ShapeDtypeStruct((n, HALF), jnp.uint32),
                   jax.ShapeDtypeStruct((n, LANES), F32), jax.ShapeDtypeStruct((1, LANES), F32)],
        scratch_shapes=[pltpu.VMEM((1, LANES), F32)],
        compiler_params=_cparams(("arbitrary",)),
        name="moe_route",
    )(x2, gla, yc, mod3, gpm, wo, mod3, mod3, gpre, w_r, tri)


def _row_copy(src, s, dst, t, sem):
    return pltpu.make_async_copy(src.at[pl.ds(s, 1), :], dst.at[pl.ds(t, 1), :], sem)


DMA_UNROLL = 8


def _dispatch_kernel(pos_ref, hp_ref, xs_in_ref, xs_ref, sem, *, tm):
    del xs_in_ref

    def start(t, carry):
        _row_copy(hp_ref, t, xs_ref, pos_ref[0, 2 * t], sem).start(priority=0)
        _row_copy(hp_ref, t, xs_ref, pos_ref[0, 2 * t + 1], sem).start(priority=1)
        return carry

    def wait(t, carry):
        _row_copy(hp_ref, 0, xs_ref, 0, sem).wait()
        return carry

    lax.fori_loop(0, tm, start, 0, unroll=DMA_UNROLL)
    lax.fori_loop(0, 2 * tm, wait, 0, unroll=DMA_UNROLL)


def _dispatch(pos3, hp, n_sorted, *, tm):
    n = hp.shape[0]
    xs0 = jnp.zeros((n_sorted, HALF), jnp.uint32)
    return pl.pallas_call(
        functools.partial(_dispatch_kernel, tm=tm),
        grid=(n // tm,),
        in_specs=[pl.BlockSpec((None, 1, 2 * tm), lambda i: (i, 0, 0), memory_space=pltpu.SMEM),
                  pl.BlockSpec((tm, HALF), lambda i: (i, 0)), pl.BlockSpec(memory_space=pl.ANY)],
        out_specs=pl.BlockSpec(memory_space=pl.ANY),
        out_shape=jax.ShapeDtypeStruct((n_sorted, HALF), jnp.uint32),
        scratch_shapes=[pltpu.SemaphoreType.DMA],
        input_output_aliases={2: 0},
        compiler_params=pltpu.CompilerParams(dimension_semantics=("arbitrary",), has_side_effects=True),
        name="moe_dispatch",
    )(pos3, hp, xs0)


def _expert_kernel(te_ref, nu_ref, xs_ref, w1_ref, w3_ref, w2_ref, i1_ref, i3_ref, i2_ref, ys_ref, *, tf):
    del te_ref

    @pl.when(pl.program_id(0) < nu_ref[0])
    def _():
        xw = xs_ref[...]
        lo = pltpu.bitcast(xw << 16, F32)
        hi = pltpu.bitcast(xw & jnp.uint32(HI_MASK), F32)
        inv = [r[:, :1] for r in (i1_ref[...], i3_ref[...], i2_ref[...])]
        ys_ref[...] = _pack_rows(_swiglu_f8([lo, hi], w1_ref, w3_ref, w2_ref, *inv, tf))


def _experts(tile_expert, n_used, xs, w1, w3, w2, *, tmg, tf):
    n_sorted = xs.shape[0]
    _, d, ff = w1.shape
    (q1, i1), (q3, i3), (q2, i2) = _quantize(w1), _quantize(w3), _quantize(w2)
    blk = lambda j, te, nu: (jnp.minimum(j, nu[0] - 1), 0)
    held = lambda r, c: pl.BlockSpec((None, r, c), lambda j, te, nu: (te[j], 0, 0), pipeline_mode=pl.Buffered(1))
    grid_spec = pltpu.PrefetchScalarGridSpec(
        num_scalar_prefetch=2,
        grid=(n_sorted // tmg,),
        in_specs=[pl.BlockSpec((tmg, HALF), blk), held(d, ff), held(d, ff), held(ff, d),
                  held(1, LANES), held(1, LANES), held(1, LANES)],
        out_specs=pl.BlockSpec((tmg, HALF), blk),
    )
    return pl.pallas_call(
        functools.partial(_expert_kernel, tf=tf),
        grid_spec=grid_spec,
        out_shape=jax.ShapeDtypeStruct((n_sorted, HALF), jnp.uint32),
        compiler_params=_cparams(("arbitrary",)),
        name="moe_experts",
    )(tile_expert, n_used, xs, q1, q3, q2, i1, i3, i2)


def _combine_kernel(pos_ref, x_ref, gt_ref, gpost_ref, info_ref, ys_ref, o_ref, buf_ref, sems, *, tm):
    i = pl.program_id(0)
    slot = i & 1

    @pl.when(i < pl.num_programs(0) - 1)
    def _():
        def start(t, carry):
            _row_copy(ys_ref, pos_ref[0, 2 * t], buf_ref.at[slot, 0], t, sems.at[slot]).start(priority=0)
            _row_copy(ys_ref, pos_ref[0, 2 * t + 1], buf_ref.at[slot, 1], t, sems.at[slot]).start(priority=1)
            return carry

        lax.fori_loop(0, tm, start, 0, unroll=DMA_UNROLL)

    @pl.when(i > 0)
    def _():
        prev = 1 - slot

        def wait(t, carry):
            _row_copy(ys_ref, 0, buf_ref.at[prev, 0], 0, sems.at[prev]).wait()
            return carry

        lax.fori_loop(0, 2 * tm, wait, 0, unroll=DMA_UNROLL)
        info = info_ref[...]
        lane = lax.broadcasted_iota(jnp.int32, info.shape, 1)
        wt1 = jnp.sum(jnp.where(lane == 2, info, 0.0), axis=-1, keepdims=True)
        wt2 = jnp.sum(jnp.where(lane == 3, info, 0.0), axis=-1, keepdims=True)
        y = wt1 * _unpack_rows(buf_ref[prev, 0]) + wt2 * _unpack_rows(buf_ref[prev, 1])
        o_ref[...] = x_ref[...] + gt_ref[...] * _rms(y, gpost_ref[...])


def _combine(pos3, x2, mod3, mod_row, gpost, info, ys, *, tm):
    n, d = x2.shape
    nt = n // tm
    done = lambda i: jnp.maximum(i - 1, 0)
    tile = pl.BlockSpec((tm, d), lambda i: (done(i), 0))
    return pl.pallas_call(
        functools.partial(_combine_kernel, tm=tm),
        grid=(nt + 1,),
        in_specs=[pl.BlockSpec((None, 1, 2 * tm), lambda i: (jnp.minimum(i, nt - 1), 0, 0),
                               memory_space=pltpu.SMEM),
                  tile, pl.BlockSpec((None, 1, d), lambda i: (mod_row(done(i)), 0, 5)),
                  pl.BlockSpec(gpost.shape, lambda i: (0, 0)),
                  pl.BlockSpec((tm, LANES), lambda i: (done(i), 0)), pl.BlockSpec(memory_space=pl.ANY)],
        out_specs=tile,
        out_shape=jax.ShapeDtypeStruct((n, d), F32),
        scratch_shapes=[pltpu.VMEM((2, 2, tm, HALF), jnp.uint32), pltpu.SemaphoreType.DMA((2,))],
        compiler_params=_cparams(("arbitrary",)),
        name="moe_combine",
    )(pos3, x2, mod3, gpost, info, ys)


def _moe_mix(x2, gla, yc, mod3, mod_row, gpm, wo, gpre, gpost, w_r, w1, w3, w2, *, tm, tmg, tf):
    n = x2.shape[0]
    x2, hp, info, cnt = _route(x2, gla, yc, mod3, mod_row, gpm, wo, gpre, w_r, tm=tm)
    counts = cnt[0, :N_EXPERTS].astype(jnp.int32)
    padded = ((counts + tmg - 1) // tmg) * tmg
    ends = jnp.cumsum(padded)
    starts = ends - padded
    n_tiles = (2 * n) // tmg + N_EXPERTS
    picks = info[:, 0:2].astype(jnp.int32)
    pos = starts[picks] + info[:, 4:6].astype(jnp.int32)
    pos3 = pos.reshape(n // tm, 1, 2 * tm)
    tile_expert = jnp.sum(jnp.arange(n_tiles, dtype=jnp.int32)[:, None] * tmg >= ends[None, :], axis=1)
    n_used = (ends[-1] // tmg).astype(jnp.int32).reshape(1)
    tile_expert = jnp.minimum(tile_expert, tile_expert[jnp.maximum(n_used[0] - 1, 0)]).astype(jnp.int32)
    xs = _dispatch(pos3, hp, n_tiles * tmg, tm=tm)
    ys = _experts(tile_expert, n_used, xs, w1, w3, w2, tmg=tmg, tf=tf)
    return _combine(pos3, x2, mod3, mod_row, gpost, info, ys, tm=tm)


def kernel(x, c, ctx, c_ctx, w_mod, b_mod, g_mix_pre, g_mix_post, w_in, w_decay, b_decay, gla_norm,
           conv_w, w_out, g_ffn_pre, g_ffn_post, w1, w3, w2, w_router, e_w1, e_w3, e_w2):
    bsz, seq, d = x.shape
    ctx_len = ctx.shape[1]
    depth = w_mod.shape[0]
    assert d == D_MODEL and ctx_len % CHUNK == 0 and bsz + 1 <= MOD_ROWS

    c_all = jnp.zeros((MOD_ROWS, d), F32).at[:bsz].set(c).at[bsz].set(c_ctx)
    mod = _modulation(c_all, w_mod, b_mod)
    consts = _gla_constants()

    tm_x = 512
    tm_p = 1024
    tm_c = ctx_len
    assert seq % tm_p == 0 and seq % tm_x == 0 and ctx_len & (ctx_len - 1) == 0
    x_row = lambda i: i // (seq // tm_x)
    c_row = lambda i: bsz
    x2 = x.reshape(bsz * seq, d)
    xc2 = ctx.reshape(bsz * ctx_len, d)
    zero_state = jnp.zeros((bsz, 2, 2 * GLA_DV, 2 * GLA_DK), F32)
    row2 = lambda a: a.reshape(1, -1)

    o_q, o_k, o_v = 0, GLA_KW, 2 * GLA_KW
    o_g = o_v + GLA_VW
    o_a = o_g + GLA_VW
    o_c = o_a + 2 * DECAY_RANK

    for i in range(depth):
        last = i == depth - 1
        mod3 = mod[i].reshape(MOD_ROWS, 1, 6 * d)
        wi = w_in[i]
        wm = wi[:, :o_a].astype(BF16)
        wa = jnp.zeros((d, LANES), F32).at[:, :2 * DECAY_RANK].set(wi[:, o_a:o_c]).astype(BF16)
        wc = wi[:, o_c:].astype(BF16)
        wd = jnp.zeros((LANES, 2 * GLA_KW), F32)
        wd = wd.at[:DECAY_RANK, :GLA_KW].set(w_decay[i, 0]).at[DECAY_RANK:2 * DECAY_RANK, GLA_KW:].set(w_decay[i, 1])
        bd = b_decay[i].reshape(1, 2 * GLA_KW)
        wo = w_out[i].astype(BF16)
        gain = row2(gla_norm[i])

        def mix(tokens, mod_row, tm, seg, nseq, slen, s0f, s0b, cps):
            q, k, v, gate, gf, gb, yc = _project(tokens, mod3, mod_row, row2(g_mix_pre[i]), wm, wa, wc, wd, bd,
                                                 conv_w[i], tm=tm, seg=seg)
            r3 = lambda a: a.reshape(nseq, slen, a.shape[-1])
            o, sf, sb = _gla(r3(q), r3(k), r3(v), r3(gf), r3(gb), r3(gate), gain, s0f, s0b, consts, cps=cps)
            return o.reshape(nseq * slen, GLA_VW), yc, sf, sb

        o_c_, yc_c, s_f, s_b = mix(xc2, c_row, tm_c, ctx_len, bsz, ctx_len, zero_state, zero_state,
                                   ctx_len // CHUNK)
        o_x, yc_x, _, _ = mix(x2, lambda t: t // (seq // tm_p), tm_p, GRID_W, bsz, seq, s_f, s_b, 8)

        j = i // 2
        common = dict(gpm=row2(g_mix_post[i]), wo=wo, gpre=row2(g_ffn_pre[i]), gpost=row2(g_ffn_post[i]))
        if i % 2 == 0:
            ffn = functools.partial(_dense_mix, w1=w1[j].astype(BF16), w3=w3[j].astype(BF16),
                                    w2=w2[j].astype(BF16), tf=MXU_TILE, **common)
        else:
            w_r = jnp.zeros((d, LANES), F32).at[:, :N_EXPERTS].set(w_router[j])
            ffn = functools.partial(_moe_mix, w_r=w_r, w1=e_w1[j], w3=e_w3[j], w2=e_w2[j], tmg=512,
                                    tf=MXU_TILE, **common)
        x2 = ffn(x2, o_x, yc_x, mod3, x_row, tm=tm_x)
        if not last:
            xc2 = ffn(xc2, o_c_, yc_c, mod3, c_row, tm=tm_c)
    return x2.reshape(bsz, seq, d)
```

```python
import functools

import numpy as np
import jax
import jax.numpy as jnp
from jax import lax
from jax.experimental import pallas as pl
from jax.experimental.pallas import tpu as pltpu

F32 = jnp.float32
BF16 = jnp.bfloat16
HIGHEST = lax.Precision.HIGHEST

D_MODEL = 1024
GLA_HEADS = 4
GLA_DK = 64
GLA_DV = 128
GLA_KW = GLA_HEADS * GLA_DK
GLA_VW = GLA_HEADS * GLA_DV
DECAY_RANK = 16
GATE_NORMALIZER = 16.0
CHUNK = 64
CONV_CH = D_MODEL - GLA_VW
GRID_W = 64
N_EXPERTS = 8
EPS = 1e-6
LANES = 128
N_LEVELS = 6
MM_LEVELS = N_LEVELS - 1
MOD_ROWS = 24
VMEM_LIMIT = 52 * 1024 * 1024


def _cparams(sem):
    return pltpu.CompilerParams(dimension_semantics=sem, vmem_limit_bytes=VMEM_LIMIT)


def _rms(x, gain):
    return x * lax.rsqrt(jnp.mean(x * x, axis=-1, keepdims=True) + EPS) * gain


def _silu(x):
    return x / (1.0 + jnp.exp(-x))


def _dot_split(a, b):
    ah = a.astype(BF16)
    al = (a - ah.astype(F32)).astype(BF16)
    bh = b.astype(BF16)
    bl = (b - bh.astype(F32)).astype(BF16)
    dot = functools.partial(jnp.dot, preferred_element_type=F32)
    return dot(ah, bh) + (dot(ah, bl) + dot(al, bh))


def _mod_kernel(c_ref, w_ref, b_ref, o_ref):
    s = _silu(c_ref[...])
    o_ref[...] = jnp.dot(s, w_ref[...], precision=HIGHEST, preferred_element_type=F32) + b_ref[...]


def _modulation(c_all, w_mod, b_mod):
    depth, d, n = w_mod.shape
    tn = 1536
    return pl.pallas_call(
        _mod_kernel,
        grid=(depth, n // tn),
        in_specs=[
            pl.BlockSpec((MOD_ROWS, d), lambda i, j: (0, 0)),
            pl.BlockSpec((None, d, tn), lambda i, j: (i, 0, j)),
            pl.BlockSpec((None, 1, tn), lambda i, j: (i, 0, j)),
        ],
        out_specs=pl.BlockSpec((None, MOD_ROWS, tn), lambda i, j: (i, 0, j)),
        out_shape=jax.ShapeDtypeStruct((depth, MOD_ROWS, n), F32),
        compiler_params=_cparams(("parallel", "parallel")),
        name="modulation",
    )(c_all, w_mod, b_mod.reshape(depth, 1, n))


def _proj_kernel(x_ref, sh_ref, sc_ref, g_ref, wm_ref, wa_ref, wc_ref, wd_ref, bd_ref, cw_ref,
                 q_ref, k_ref, v_ref, gate_ref, gf_ref, gb_ref, yc_ref, *, seg, sub):
    for r0 in range(0, x_ref.shape[0], sub):
        rows = slice(r0, r0 + sub)
        h = _rms(x_ref[rows, :], g_ref[...]) * (1.0 + sc_ref[...]) + sh_ref[...]
        hb = h.astype(BF16)
        tile = lambda c0: jnp.dot(hb, wm_ref[:, c0:c0 + MXU_TILE], preferred_element_type=F32)
        q_ref[rows, :] = (tile(0) * (GLA_DK ** -0.5)).astype(BF16)
        k_ref[rows, :] = tile(GLA_KW).astype(BF16)
        for c0 in range(0, GLA_VW, MXU_TILE):
            v_ref[rows, c0:c0 + MXU_TILE] = tile(2 * GLA_KW + c0).astype(BF16)
            gate_ref[rows, c0:c0 + MXU_TILE] = tile(2 * GLA_KW + GLA_VW + c0).astype(BF16)
        pa = jnp.dot(hb, wa_ref[...], preferred_element_type=F32)
        xd = _dot_split(pa, wd_ref[...]) + bd_ref[...]
        ls = (jnp.minimum(xd, 0.0) - jnp.log1p(jnp.exp(-jnp.abs(xd)))) * (LOG2E / GATE_NORMALIZER)
        gf_ref[rows, :] = ls[:, :GLA_KW]
        gb_ref[rows, :] = ls[:, GLA_KW:]
        row = lax.broadcasted_iota(jnp.int32, (sub, MXU_TILE), 0) & (seg - 1)
        for c0 in range(0, CONV_CH, MXU_TILE):
            cols = slice(c0, c0 + MXU_TILE)
            part = lambda j: jnp.dot(hb, wc_ref[:, j * CONV_CH + c0:j * CONV_CH + c0 + MXU_TILE],
                                     preferred_element_type=F32)
            u = part(1) * part(2)
            u_prev = jnp.where(row == 0, 0.0, pltpu.roll(u, 1, 0))
            u_next = jnp.where(row == seg - 1, 0.0, pltpu.roll(u, sub - 1, 0))
            cw = cw_ref[:, cols]
            yc = part(0) * (cw[0:1] * u_prev + cw[1:2] * u + cw[2:3] * u_next)
            yc_ref[rows, cols] = yc.astype(BF16)


def _project(x2, mod3, mod_row, gain, wm, wa, wc, wd, bd, cw, *, tm, seg):
    n = x2.shape[0]
    d = D_MODEL
    row = lambda c: pl.BlockSpec((None, 1, d), lambda i: (mod_row(i), 0, c))
    const = lambda a: pl.BlockSpec(a.shape, lambda i: (0,) * a.ndim, pipeline_mode=pl.Buffered(1))
    tile = lambda w: pl.BlockSpec((tm, w), lambda i: (i, 0))
    shapes = [(GLA_KW, BF16), (GLA_KW, BF16), (GLA_VW, BF16), (GLA_VW, BF16),
              (GLA_KW, F32), (GLA_KW, F32), (CONV_CH, BF16)]
    return pl.pallas_call(
        functools.partial(_proj_kernel, seg=seg, sub=min(tm, 512)),
        grid=(n // tm,),
        in_specs=[tile(d), row(0), row(1), const(gain), const(wm), const(wa), const(wc),
                  const(wd), const(bd), const(cw)],
        out_specs=[tile(w) for w, _ in shapes],
        out_shape=[jax.ShapeDtypeStruct((n, w), t) for w, t in shapes],
        compiler_params=_cparams(("parallel",)),
        name="in_proj",
    )(x2, mod3, mod3, gain, wm, wa, wc, wd, bd, cw)


def _gla_constants():
    c = CHUNK
    main = np.zeros((2 * MM_LEVELS + 2, c, 2 * c), np.float32)
    lmask = np.zeros((N_LEVELS + 1, c, c), np.float32)
    for lvl in range(N_LEVELS):
        s = c >> (lvl + 1)
        for i in range(c):
            mid = (i // (2 * s)) * 2 * s + s
            if lvl < MM_LEVELS and i >= mid:
                main[2 * lvl, i, mid:i + 1] = 1.0
                main[2 * lvl + 1, i, c + mid:c + i] = 1.0
            elif lvl < MM_LEVELS:
                main[2 * lvl, i, c + i:c + mid] = 1.0
                main[2 * lvl + 1, i, i + 1:mid] = 1.0
            for j in range(c):
                same = (j // (2 * s)) == (i // (2 * s))
                lmask[lvl, i, j] = float(same and ((i >= mid) != (j >= mid)))
    lmask[N_LEVELS] = 2.0 * np.eye(c)
    bwd = np.zeros((2, c, c), np.float32)
    for i in range(c):
        main[2 * MM_LEVELS, i, :i + 1] = 1.0
        main[2 * MM_LEVELS + 1, i, i + 1:c] = 1.0
        bwd[0, i, i:] = 1.0
        bwd[1, i, :i] = 1.0
    main = main.reshape(-1, 2 * c)
    bwd = bwd.reshape(-1, c)
    lmask = np.concatenate([lmask, lmask], axis=-1)
    return jnp.asarray(main, BF16), jnp.asarray(bwd, BF16), jnp.asarray(lmask, F32)


def _sum_dot(n_ref, g, parts):
    gb = g.astype(BF16)
    rows = n_ref.shape[0] // parts
    outs = [jnp.dot(n_ref[p * rows:(p + 1) * rows, :], gb, preferred_element_type=F32) for p in range(parts)]
    per = rows // CHUNK
    return lambda b: outs[b // per][(b % per) * CHUNK:(b % per + 1) * CHUNK]


_NT = (((1,), (1,)), ((), ()))
_TN = (((0,), (0,)), ((), ()))


def _gla_kernel(qf_ref, kf_ref, vf_ref, gff_ref, gbf_ref, qb_ref, kb_ref, vb_ref, gbb_ref,
                gate_ref, gain_ref, s0f_ref, s0b_ref, nmain_ref, nbwd_ref, lmask_ref,
                out_ref, sf_out_ref, sb_out_ref, acc_ref, sf_ref, sb_ref, *, cps):
    n = pl.program_id(1)
    nb = pl.num_programs(1)
    c = CHUNK
    pw = 2 * GLA_DK
    vw = 2 * GLA_DV

    @pl.when(n == 0)
    def _():
        acc_ref[...] = jnp.zeros_like(acc_ref)
        sf_ref[...] = s0f_ref[...]
        sb_ref[...] = s0b_ref[...]

    klane = lax.broadcasted_iota(jnp.int32, (c, GLA_KW), 1)
    k_even = (klane & (pw - 1)) < GLA_DK
    vlane = lax.broadcasted_iota(jnp.int32, (c, vw), 1)
    v_low = vlane < GLA_DV
    srow = lax.broadcasted_iota(jnp.int32, (vw, pw), 0)
    scol = lax.broadcasted_iota(jnp.int32, (vw, pw), 1)
    s_diag = (srow < GLA_DV) == (scol < GLA_DK)
    zero_b = jnp.zeros((), BF16)

    def inter(qi, ki, dec, v, s_ref):
        outs = []
        for p in range(2):
            st = s_ref[p]
            outs.append(lax.dot_general(qi[:, p * pw:(p + 1) * pw], st.astype(BF16), _NT,
                                        preferred_element_type=F32))
            upd = lax.dot_general(v[:, p * vw:(p + 1) * vw], ki[:, p * pw:(p + 1) * pw], _TN,
                                  preferred_element_type=F32)
            s_ref[p] = st * dec[:, p * pw:(p + 1) * pw] + jnp.where(s_diag, upd, 0.0)
        return jnp.concatenate(outs, axis=1)

    odd_row = (lax.broadcasted_iota(jnp.int32, (c, GLA_KW), 0) & 1) == 1

    def fwd_chunk(j):
        r0 = pl.multiple_of(j * c, c)
        rows = pl.ds(r0, c)
        qb16 = qf_ref[rows, :]
        kb16 = kf_ref[rows, :]
        v = vf_ref[rows, :]
        q = qb16.astype(F32)
        k = kb16.astype(F32)
        gf = gff_ref[rows, :]
        gb = gbf_ref[rows, :]
        e = _sum_dot(nmain_ref, jnp.concatenate([gf, gb], axis=0), 2)
        att = [jnp.zeros((c, pw), F32), jnp.zeros((c, pw), F32)]
        for lvl in range(N_LEVELS + 1):
            if lvl < MM_LEVELS:
                qt = (q * jnp.exp2(e(2 * lvl))).astype(BF16)
                kt = (k * jnp.exp2(e(2 * lvl + 1))).astype(BF16)
            elif lvl < N_LEVELS:
                qt, kt = (q * jnp.exp2(jnp.where(odd_row, gf, gb))).astype(BF16), kb16
            else:
                qt, kt = qb16, kb16
            ke = jnp.where(k_even, kt, zero_b)
            ko = jnp.where(k_even, zero_b, kt)
            m = lmask_ref[lvl]
            for p in range(2):
                kbd = jnp.concatenate([ke[:, p * pw:(p + 1) * pw], ko[:, p * pw:(p + 1) * pw]], axis=0)
                att[p] = att[p] + m * lax.dot_general(qt[:, p * pw:(p + 1) * pw], kbd, _NT,
                                                      preferred_element_type=F32)
        outs = []
        for p in range(2):
            vp = v[:, p * vw:(p + 1) * vw]
            vbd = jnp.concatenate([jnp.where(v_low, vp, zero_b), jnp.where(v_low, zero_b, vp)], axis=0)
            outs.append(jnp.dot(att[p].astype(BF16), vbd, preferred_element_type=F32))
        o = jnp.concatenate(outs, axis=1)
        cum = e(2 * MM_LEVELS)
        qi = (q * jnp.exp2(cum)).astype(BF16)
        ki = (k * jnp.exp2(e(2 * MM_LEVELS + 1))).astype(BF16)
        o = o + inter(qi, ki, jnp.exp2(cum[c - 1:c]), v, sf_ref)
        a0 = pl.multiple_of(n * (cps * c) + r0, c)
        acc_ref[pl.ds(a0, c), :] += o

    def bwd_chunk(j):
        r0 = pl.multiple_of((cps - 1 - j) * c, c)
        rows = pl.ds(r0, c)
        q = qb_ref[rows, :].astype(F32)
        k = kb_ref[rows, :].astype(F32)
        v = vb_ref[rows, :]
        e = _sum_dot(nbwd_ref, gbb_ref[rows, :], 1)
        cum = e(0)
        qi = (q * jnp.exp2(cum)).astype(BF16)
        ki = (k * jnp.exp2(e(1))).astype(BF16)
        o = inter(qi, ki, jnp.exp2(cum[0:1]), v, sb_ref)
        a0 = pl.multiple_of((nb - 1 - n) * (cps * c) + r0, c)
        acc_ref[pl.ds(a0, c), :] += o

    def both(j, carry):
        fwd_chunk(j)
        bwd_chunk(j)
        return carry

    lax.fori_loop(0, cps, both, 0, unroll=min(cps, 4))

    @pl.when(n == nb - 1)
    def _():
        sf_out_ref[...] = sf_ref[...]
        sb_out_ref[...] = sb_ref[...]
        gain = gain_ref[...]
        rb = 256

        def gate_rows(i, carry):
            rows = pl.ds(pl.multiple_of(i * rb, rb), rb)
            o = acc_ref[rows, :]
            gt = gate_ref[rows, :].astype(F32)
            normed = jnp.concatenate(
                [_rms(o[:, h * GLA_DV:(h + 1) * GLA_DV], gain) for h in range(GLA_HEADS)], axis=1)
            out_ref[rows, :] = (normed * _silu(gt)).astype(BF16)
            return carry

        lax.fori_loop(0, acc_ref.shape[0] // rb, gate_rows, 0)


def _gla(q, k, v, gf, gb, gate, gain, s0f, s0b, consts, *, cps):
    b, l, _ = q.shape
    t = cps * CHUNK
    nb = l // t
    nmain, nbwd, lmask = consts
    fwd = lambda w: pl.BlockSpec((None, t, w), lambda i, n: (i, n, 0))
    bwd = lambda w: pl.BlockSpec((None, t, w), lambda i, n: (i, nb - 1 - n, 0))
    whole = lambda w: pl.BlockSpec((None, l, w), lambda i, n: (i, 0, 0))
    state = pl.BlockSpec((None, 2, 2 * GLA_DV, 2 * GLA_DK), lambda i, n: (i, 0, 0, 0))
    const = lambda a: pl.BlockSpec(a.shape, lambda i, n: (0,) * a.ndim)
    sshape = jax.ShapeDtypeStruct((b, 2, 2 * GLA_DV, 2 * GLA_DK), F32)
    return pl.pallas_call(
        functools.partial(_gla_kernel, cps=cps),
        grid=(b, nb),
        in_specs=[fwd(GLA_KW), fwd(GLA_KW), fwd(GLA_VW), fwd(GLA_KW), fwd(GLA_KW),
                  bwd(GLA_KW), bwd(GLA_KW), bwd(GLA_VW), bwd(GLA_KW),
                  whole(GLA_VW), const(gain), state, state, const(nmain), const(nbwd), const(lmask)],
        out_specs=[whole(GLA_VW), state, state],
        out_shape=[jax.ShapeDtypeStruct((b, l, GLA_VW), BF16), sshape, sshape],
        scratch_shapes=[pltpu.VMEM((l, GLA_VW), F32),
                        pltpu.VMEM((2, 2 * GLA_DV, 2 * GLA_DK), F32),
                        pltpu.VMEM((2, 2 * GLA_DV, 2 * GLA_DK), F32)],
        compiler_params=_cparams(("parallel", "arbitrary")),
        name="gla_scan",
    )(q, k, v, gf, gb, q, k, v, gb, gate, gain, s0f, s0b, nmain, nbwd, lmask)


MXU_TILE = 256


def _ff_chunks(ff, tf):
    return [(f0, min(f0 + tf, ff)) for f0 in range(0, ff, tf)]


def _swiglu(x, w1_ref, w3_ref, w2_ref, tf):
    ms = []
    for f0, f1 in _ff_chunks(w1_ref.shape[1], tf):
        a = jnp.dot(x, w1_ref[:, f0:f1], preferred_element_type=F32)
        b = jnp.dot(x, w3_ref[:, f0:f1], preferred_element_type=F32)
        ms.append((_silu(a) * b).astype(BF16))
    return jnp.dot(jnp.concatenate(ms, axis=1), w2_ref[...], preferred_element_type=F32)


F8 = jnp.float8_e4m3fn
F8_PEAK = 256.0
TINY = 1e-30
LOG2E = 1.4426950408889634


def _quant_kernel(w_ref, q_ref, inv_ref, *, rb):
    nblk = w_ref.shape[0] // rb

    def peak(i, m):
        blk = jnp.abs(w_ref[pl.ds(pl.multiple_of(i * rb, rb), rb), :])
        return jnp.maximum(m, jnp.max(blk, axis=0, keepdims=True))

    colmax = lax.fori_loop(0, nblk, peak, jnp.zeros((1, w_ref.shape[1]), F32))
    amax = jnp.maximum(jnp.max(colmax, axis=1, keepdims=True), TINY)
    scale = F8_PEAK / amax

    def cast(i, carry):
        rows = pl.ds(pl.multiple_of(i * rb, rb), rb)
        q_ref[rows, :] = (w_ref[rows, :] * scale).astype(F8)
        return carry

    lax.fori_loop(0, nblk, cast, 0)
    inv_ref[...] = jnp.broadcast_to(amax * (1.0 / F8_PEAK), inv_ref.shape)


def _quantize(w):
    ne, r, c = w.shape
    return pl.pallas_call(
        functools.partial(_quant_kernel, rb=128),
        grid=(ne,),
        in_specs=[pl.BlockSpec((None, r, c), lambda e: (e, 0, 0))],
        out_specs=[pl.BlockSpec((None, r, c), lambda e: (e, 0, 0)), pl.BlockSpec((None, 1, LANES), lambda e: (e, 0, 0))],
        out_shape=[jax.ShapeDtypeStruct((ne, r, c), F8), jax.ShapeDtypeStruct((ne, 1, LANES), F32)],
        compiler_params=_cparams(("parallel",)),
        name="quantize_weights",
    )(w)


def _row_scale(parts):
    amax = functools.reduce(jnp.maximum, [jnp.max(jnp.abs(p), axis=1, keepdims=True) for p in parts])
    amax = jnp.maximum(amax, TINY)
    return F8_PEAK / amax, amax * (1.0 / F8_PEAK)


def _swiglu_f8(xs, w1_ref, w3_ref, w2_ref, inv1, inv3, inv2, tf):
    k = xs[0].shape[1]
    sx, ix = _row_scale(xs)
    x8 = [(x * sx).astype(F8) for x in xs]
    c1 = ix * inv1
    ms = []
    for f0, f1 in _ff_chunks(w1_ref.shape[1], tf):
        a = sum(jnp.dot(x, w1_ref[r * k:(r + 1) * k, f0:f1], preferred_element_type=F32) for r, x in enumerate(x8))
        b = sum(jnp.dot(x, w3_ref[r * k:(r + 1) * k, f0:f1], preferred_element_type=F32) for r, x in enumerate(x8))
        ms.append(a * b / (1.0 + jnp.exp2(a * (c1 * -LOG2E))))
    sm, im = _row_scale(ms)
    m8 = jnp.concatenate([(m * sm).astype(F8) for m in ms], axis=1)
    return jnp.dot(m8, w2_ref[...], preferred_element_type=F32) * (im * c1 * (ix * inv3) * inv2)


def _token_mix_residual(x_ref, gla_ref, yc_ref, gt_ref, gp_ref, wo_ref, rows=slice(None)):
    y = (jnp.dot(gla_ref[rows, :], wo_ref[:GLA_VW, :], preferred_element_type=F32)
         + jnp.dot(yc_ref[rows, :], wo_ref[GLA_VW:, :], preferred_element_type=F32))
    return x_ref[rows, :] + gt_ref[...] * _rms(y, gp_ref[...])


def _mix_specs(mod_row, gpm, wo, tm):
    d = D_MODEL
    return [pl.BlockSpec((tm, d), lambda i: (i, 0)), pl.BlockSpec((tm, GLA_VW), lambda i: (i, 0)),
            pl.BlockSpec((tm, CONV_CH), lambda i: (i, 0)),
            pl.BlockSpec((None, 1, d), lambda i: (mod_row(i), 0, 2)),
            pl.BlockSpec(gpm.shape, lambda i: (0, 0)),
            pl.BlockSpec(wo.shape, lambda i: (0, 0), pipeline_mode=pl.Buffered(1))]


def _dense_kernel(x_ref, gla_ref, yc_ref, gt1_ref, gpm_ref, wo_ref, sh_ref, sc_ref, gt_ref, gpre_ref, gpost_ref,
                  w1_ref, w3_ref, w2_ref, o_ref, *, tf, sub):
    for r0 in range(0, x_ref.shape[0], sub):
        rows = slice(r0, r0 + sub)
        x1 = _token_mix_residual(x_ref, gla_ref, yc_ref, gt1_ref, gpm_ref, wo_ref, rows)
        h = _rms(x1, gpre_ref[...]) * (1.0 + sc_ref[...]) + sh_ref[...]
        y = _swiglu(h.astype(BF16), w1_ref, w3_ref, w2_ref, tf)
        o_ref[rows, :] = x1 + gt_ref[...] * _rms(y, gpost_ref[...])


def _dense_mix(x2, gla, yc, mod3, mod_row, gpm, wo, gpre, gpost, w1, w3, w2, *, tm, tf):
    n, d = x2.shape
    row = lambda c: pl.BlockSpec((None, 1, d), lambda i: (mod_row(i), 0, c))
    const = lambda a: pl.BlockSpec(a.shape, lambda i: (0,) * a.ndim)
    held = lambda a: pl.BlockSpec(a.shape, lambda i: (0,) * a.ndim, pipeline_mode=pl.Buffered(1))
    return pl.pallas_call(
        functools.partial(_dense_kernel, tf=tf, sub=min(tm, 512)),
        grid=(n // tm,),
        in_specs=_mix_specs(mod_row, gpm, wo, tm) + [row(3), row(4), row(5), const(gpre), const(gpost),
                                                     held(w1), held(w3), held(w2)],
        out_specs=pl.BlockSpec((tm, d), lambda i: (i, 0)),
        out_shape=jax.ShapeDtypeStruct((n, d), F32),
        compiler_params=_cparams(("parallel",)),
        name="dense_mix",
    )(x2, gla, yc, mod3, gpm, wo, mod3, mod3, mod3, gpre, gpost, w1, w3, w2)


HALF = D_MODEL // 2
HI_MASK = 0xFFFF0000


def _pack_rows(v):
    bits = pltpu.bitcast(v.astype(BF16).astype(F32), jnp.uint32)
    return (bits[:, :HALF] >> 16) | (bits[:, HALF:] & jnp.uint32(HI_MASK))


def _unpack_rows(w):
    lo = pltpu.bitcast(w << 16, F32)
    hi = pltpu.bitcast(w & jnp.uint32(HI_MASK), F32)
    return jnp.concatenate([lo, hi], axis=1)


def _route_kernel(x_ref, gla_ref, yc_ref, gt1_ref, gpm_ref, wo_ref, sh_ref, sc_ref, gpre_ref, wr_ref, tri_ref,
                  x1_ref, hp_ref, info_ref, cnt_ref, carry_ref):
    i = pl.program_id(0)

    @pl.when(i == 0)
    def _():
        carry_ref[...] = jnp.zeros_like(carry_ref)

    sub = tri_ref.shape[0]
    carry = carry_ref[...]
    for r0 in range(0, x_ref.shape[0], sub):
        rows = slice(r0, r0 + sub)
        x1 = _token_mix_residual(x_ref, gla_ref, yc_ref, gt1_ref, gpm_ref, wo_ref, rows)
        x1_ref[rows, :] = x1
        h = _rms(x1, gpre_ref[...]) * (1.0 + sc_ref[...]) + sh_ref[...]
        hp_ref[rows, :] = _pack_rows(h)
        logits = _dot_split(h, wr_ref[...])
        lane = lax.broadcasted_iota(jnp.int32, logits.shape, 1).astype(F32)
        logits = jnp.where(lane < N_EXPERTS, logits, -jnp.inf)
        m1 = jnp.max(logits, axis=-1, keepdims=True)
        i1 = jnp.min(jnp.where(logits == m1, lane, float(LANES)), axis=-1, keepdims=True)
        rest = jnp.where(lane == i1, -jnp.inf, logits)
        m2 = jnp.max(rest, axis=-1, keepdims=True)
        i2 = jnp.min(jnp.where(rest == m2, lane, float(LANES)), axis=-1, keepdims=True)
        e2 = jnp.exp(m2 - m1)
        den = 1.0 + e2
        pick = jnp.where((lane == i1) | (lane == i2), 1.0, 0.0)
        rank = jnp.dot(tri_ref[...], pick.astype(BF16), preferred_element_type=F32) + carry
        carry = carry + jnp.sum(pick, axis=0, keepdims=True)
        r1 = jnp.sum(jnp.where(lane == i1, rank, 0.0), axis=-1, keepdims=True)
        r2 = jnp.sum(jnp.where(lane == i2, rank, 0.0), axis=-1, keepdims=True)
        info = jnp.zeros_like(logits)
        for col, val in enumerate((i1, i2, 1.0 / den, e2 / den, r1, r2)):
            info = jnp.where(lane == col, val, info)
        info_ref[rows, :] = info
    carry_ref[...] = carry
    cnt_ref[...] = carry


def _route(x2, gla, yc, mod3, mod_row, gpm, wo, gpre, w_r, *, tm):
    n, d = x2.shape
    row = lambda c: pl.BlockSpec((None, 1, d), lambda i: (mod_row(i), 0, c))
    const = lambda a: pl.BlockSpec(a.shape, lambda i: (0,) * a.ndim)
    sub = min(tm, 256)
    tri = jnp.asarray(np.tril(np.ones((sub, sub), np.float32), -1), BF16)
    return pl.pallas_call(
        _route_kernel,
        grid=(n // tm,),
        in_specs=_mix_specs(mod_row, gpm, wo, tm) + [row(3), row(4), const(gpre), const(w_r), const(tri)],
        out_specs=[pl.BlockSpec((tm, d), lambda i: (i, 0)), pl.BlockSpec((tm, HALF), lambda i: (i, 0)),
                   pl.BlockSpec((tm, LANES), lambda i: (i, 0)), pl.BlockSpec((1, LANES), lambda i: (0, 0))],
        out_shape=[jax.ShapeDtypeStruct((n, d), F32), jax.ShapeDtypeStruct((n, HALF), jnp.uint32),
                   jax.ShapeDtypeStruct((n, LANES), F32), jax.ShapeDtypeStruct((1, LANES), F32)],
        scratch_shapes=[pltpu.VMEM((1, LANES), F32)],
        compiler_params=_cparams(("arbitrary",)),
        name="moe_route",
    )(x2, gla, yc, mod3, gpm, wo, mod3, mod3, gpre, w_r, tri)


def _row_copy(src, s, dst, t, sem):
    return pltpu.make_async_copy(src.at[pl.ds(s, 1), :], dst.at[pl.ds(t, 1), :], sem)


DMA_UNROLL = 8


def _dispatch_kernel(pos_ref, hp_ref, xs_in_ref, xs_ref, sem, *, tm):
    del xs_in_ref

    def start(t, carry):
        _row_copy(hp_ref, t, xs_ref, pos_ref[0, 2 * t], sem).start(priority=0)
        _row_copy(hp_ref, t, xs_ref, pos_ref[0, 2 * t + 1], sem).start(priority=1)
        return carry

    def wait(t, carry):
        _row_copy(hp_ref, 0, xs_ref, 0, sem).wait()
        return carry

    lax.fori_loop(0, tm, start, 0, unroll=DMA_UNROLL)
    lax.fori_loop(0, 2 * tm, wait, 0, unroll=DMA_UNROLL)


def _dispatch(pos3, hp, n_sorted, *, tm):
    n = hp.shape[0]
    xs0 = jnp.zeros((n_sorted, HALF), jnp.uint32)
    return pl.pallas_call(
        functools.partial(_dispatch_kernel, tm=tm),
        grid=(n // tm,),
        in_specs=[pl.BlockSpec((None, 1, 2 * tm), lambda i: (i, 0, 0), memory_space=pltpu.SMEM),
                  pl.BlockSpec((tm, HALF), lambda i: (i, 0)), pl.BlockSpec(memory_space=pl.ANY)],
        out_specs=pl.BlockSpec(memory_space=pl.ANY),
        out_shape=jax.ShapeDtypeStruct((n_sorted, HALF), jnp.uint32),
        scratch_shapes=[pltpu.SemaphoreType.DMA],
        input_output_aliases={2: 0},
        compiler_params=pltpu.CompilerParams(dimension_semantics=("arbitrary",), has_side_effects=True),
        name="moe_dispatch",
    )(pos3, hp, xs0)


def _expert_kernel(te_ref, nu_ref, xs_ref, w1_ref, w3_ref, w2_ref, i1_ref, i3_ref, i2_ref, ys_ref, *, tf):
    del te_ref

    @pl.when(pl.program_id(0) < nu_ref[0])
    def _():
        xw = xs_ref[...]
        lo = pltpu.bitcast(xw << 16, F32)
        hi = pltpu.bitcast(xw & jnp.uint32(HI_MASK), F32)
        inv = [r[:, :1] for r in (i1_ref[...], i3_ref[...], i2_ref[...])]
        ys_ref[...] = _pack_rows(_swiglu_f8([lo, hi], w1_ref, w3_ref, w2_ref, *inv, tf))


def _experts(tile_expert, n_used, xs, w1, w3, w2, *, tmg, tf):
    n_sorted = xs.shape[0]
    _, d, ff = w1.shape
    (q1, i1), (q3, i3), (q2, i2) = _quantize(w1), _quantize(w3), _quantize(w2)
    blk = lambda j, te, nu: (jnp.minimum(j, nu[0] - 1), 0)
    held = lambda r, c: pl.BlockSpec((None, r, c), lambda j, te, nu: (te[j], 0, 0), pipeline_mode=pl.Buffered(1))
    grid_spec = pltpu.PrefetchScalarGridSpec(
        num_scalar_prefetch=2,
        grid=(n_sorted // tmg,),
        in_specs=[pl.BlockSpec((tmg, HALF), blk), held(d, ff), held(d, ff), held(ff, d),
                  held(1, LANES), held(1, LANES), held(1, LANES)],
        out_specs=pl.BlockSpec((tmg, HALF), blk),
    )
    return pl.pallas_call(
        functools.partial(_expert_kernel, tf=tf),
        grid_spec=grid_spec,
        out_shape=jax.ShapeDtypeStruct((n_sorted, HALF), jnp.uint32),
        compiler_params=_cparams(("arbitrary",)),
        name="moe_experts",
    )(tile_expert, n_used, xs, q1, q3, q2, i1, i3, i2)


def _combine_kernel(pos_ref, x_ref, gt_ref, gpost_ref, info_ref, ys_ref, o_ref, buf_ref, sems, *, tm):
    i = pl.program_id(0)
    slot = i & 1

    @pl.when(i < pl.num_programs(0) - 1)
    def _():
        def start(t, carry):
            _row_copy(ys_ref, pos_ref[0, 2 * t], buf_ref.at[slot, 0], t, sems.at[slot]).start(priority=0)
            _row_copy(ys_ref, pos_ref[0, 2 * t + 1], buf_ref.at[slot, 1], t, sems.at[slot]).start(priority=1)
            return carry

        lax.fori_loop(0, tm, start, 0, unroll=DMA_UNROLL)

    @pl.when(i > 0)
    def _():
        prev = 1 - slot

        def wait(t, carry):
            _row_copy(ys_ref, 0, buf_ref.at[prev, 0], 0, sems.at[prev]).wait()
            return carry

        lax.fori_loop(0, 2 * tm, wait, 0, unroll=DMA_UNROLL)
        info = info_ref[...]
        lane = lax.broadcasted_iota(jnp.int32, info.shape, 1)
        wt1 = jnp.sum(jnp.where(lane == 2, info, 0.0), axis=-1, keepdims=True)
        wt2 = jnp.sum(jnp.where(lane == 3, info, 0.0), axis=-1, keepdims=True)
        y = wt1 * _unpack_rows(buf_ref[prev, 0]) + wt2 * _unpack_rows(buf_ref[prev, 1])
        o_ref[...] = x_ref[...] + gt_ref[...] * _rms(y, gpost_ref[...])


def _combine(pos3, x2, mod3, mod_row, gpost, info, ys, *, tm):
    n, d = x2.shape
    nt = n // tm
    done = lambda i: jnp.maximum(i - 1, 0)
    tile = pl.BlockSpec((tm, d), lambda i: (done(i), 0))
    return pl.pallas_call(
        functools.partial(_combine_kernel, tm=tm),
        grid=(nt + 1,),
        in_specs=[pl.BlockSpec((None, 1, 2 * tm), lambda i: (jnp.minimum(i, nt - 1), 0, 0),
                               memory_space=pltpu.SMEM),
                  tile, pl.BlockSpec((None, 1, d), lambda i: (mod_row(done(i)), 0, 5)),
                  pl.BlockSpec(gpost.shape, lambda i: (0, 0)),
                  pl.BlockSpec((tm, LANES), lambda i: (done(i), 0)), pl.BlockSpec(memory_space=pl.ANY)],
        out_specs=tile,
        out_shape=jax.ShapeDtypeStruct((n, d), F32),
        scratch_shapes=[pltpu.VMEM((2, 2, tm, HALF), jnp.uint32), pltpu.SemaphoreType.DMA((2,))],
        compiler_params=_cparams(("arbitrary",)),
        name="moe_combine",
    )(pos3, x2, mod3, gpost, info, ys)


def _moe_mix(x2, gla, yc, mod3, mod_row, gpm, wo, gpre, gpost, w_r, w1, w3, w2, *, tm, tmg, tf):
    n = x2.shape[0]
    x2, hp, info, cnt = _route(x2, gla, yc, mod3, mod_row, gpm, wo, gpre, w_r, tm=tm)
    counts = cnt[0, :N_EXPERTS].astype(jnp.int32)
    padded = ((counts + tmg - 1) // tmg) * tmg
    ends = jnp.cumsum(padded)
    starts = ends - padded
    n_tiles = (2 * n) // tmg + N_EXPERTS
    picks = info[:, 0:2].astype(jnp.int32)
    pos = starts[picks] + info[:, 4:6].astype(jnp.int32)
    pos3 = pos.reshape(n // tm, 1, 2 * tm)
    tile_expert = jnp.sum(jnp.arange(n_tiles, dtype=jnp.int32)[:, None] * tmg >= ends[None, :], axis=1)
    n_used = (ends[-1] // tmg).astype(jnp.int32).reshape(1)
    tile_expert = jnp.minimum(tile_expert, tile_expert[jnp.maximum(n_used[0] - 1, 0)]).astype(jnp.int32)
    xs = _dispatch(pos3, hp, n_tiles * tmg, tm=tm)
    ys = _experts(tile_expert, n_used, xs, w1, w3, w2, tmg=tmg, tf=tf)
    return _combine(pos3, x2, mod3, mod_row, gpost, info, ys, tm=tm)


def kernel(x, c, ctx, c_ctx, w_mod, b_mod, g_mix_pre, g_mix_post, w_in, w_decay, b_decay, gla_norm,
           conv_w, w_out, g_ffn_pre, g_ffn_post, w1, w3, w2, w_router, e_w1, e_w3, e_w2):
    bsz, seq, d = x.shape
    ctx_len = ctx.shape[1]
    depth = w_mod.shape[0]
    assert d == D_MODEL and ctx_len % CHUNK == 0 and bsz + 1 <= MOD_ROWS

    c_all = jnp.zeros((MOD_ROWS, d), F32).at[:bsz].set(c).at[bsz].set(c_ctx)
    mod = _modulation(c_all, w_mod, b_mod)
    consts = _gla_constants()

    tm_x = 512
    tm_p = 1024
    tm_c = ctx_len
    assert seq % tm_p == 0 and seq % tm_x == 0 and ctx_len & (ctx_len - 1) == 0
    x_row = lambda i: i // (seq // tm_x)
    c_row = lambda i: bsz
    x2 = x.reshape(bsz * seq, d)
    xc2 = ctx.reshape(bsz * ctx_len, d)
    zero_state = jnp.zeros((bsz, 2, 2 * GLA_DV, 2 * GLA_DK), F32)
    row2 = lambda a: a.reshape(1, -1)

    o_q, o_k, o_v = 0, GLA_KW, 2 * GLA_KW
    o_g = o_v + GLA_VW
    o_a = o_g + GLA_VW
    o_c = o_a + 2 * DECAY_RANK

    for i in range(depth):
        last = i == depth - 1
        mod3 = mod[i].reshape(MOD_ROWS, 1, 6 * d)
        wi = w_in[i]
        wm = wi[:, :o_a].astype(BF16)
        wa = jnp.zeros((d, LANES), F32).at[:, :2 * DECAY_RANK].set(wi[:, o_a:o_c]).astype(BF16)
        wc = wi[:, o_c:].astype(BF16)
        wd = jnp.zeros((LANES, 2 * GLA_KW), F32)
        wd = wd.at[:DECAY_RANK, :GLA_KW].set(w_decay[i, 0]).at[DECAY_RANK:2 * DECAY_RANK, GLA_KW:].set(w_decay[i, 1])
        bd = b_decay[i].reshape(1, 2 * GLA_KW)
        wo = w_out[i].astype(BF16)
        gain = row2(gla_norm[i])

        def mix(tokens, mod_row, tm, seg, nseq, slen, s0f, s0b, cps):
            q, k, v, gate, gf, gb, yc = _project(tokens, mod3, mod_row, row2(g_mix_pre[i]), wm, wa, wc, wd, bd,
                                                 conv_w[i], tm=tm, seg=seg)
            r3 = lambda a: a.reshape(nseq, slen, a.shape[-1])
            o, sf, sb = _gla(r3(q), r3(k), r3(v), r3(gf), r3(gb), r3(gate), gain, s0f, s0b, consts, cps=cps)
            return o.reshape(nseq * slen, GLA_VW), yc, sf, sb

        o_c_, yc_c, s_f, s_b = mix(xc2, c_row, tm_c, ctx_len, bsz, ctx_len, zero_state, zero_state,
                                   ctx_len // CHUNK)
        o_x, yc_x, _, _ = mix(x2, lambda t: t // (seq // tm_p), tm_p, GRID_W, bsz, seq, s_f, s_b, 8)

        j = i // 2
        common = dict(gpm=row2(g_mix_post[i]), wo=wo, gpre=row2(g_ffn_pre[i]), gpost=row2(g_ffn_post[i]))
        if i % 2 == 0:
            ffn = functools.partial(_dense_mix, w1=w1[j].astype(BF16), w3=w3[j].astype(BF16),
                                    w2=w2[j].astype(BF16), tf=MXU_TILE, **common)
        else:
            w_r = jnp.zeros((d, LANES), F32).at[:, :N_EXPERTS].set(w_router[j])
            ffn = functools.partial(_moe_mix, w_r=w_r, w1=e_w1[j], w3=e_w3[j], w2=e_w2[j], tmg=512,
                                    tf=MXU_TILE, **common)
        x2 = ffn(x2, o_x, yc_x, mod3, x_row, tm=tm_x)
        if not last:
            xc2 = ffn(xc2, o_c_, yc_c, mod3, c_row, tm=tm_c)
    return x2.reshape(bsz, seq, d)
```

```python
import functools

import numpy as np
import jax
import jax.numpy as jnp
from jax import lax
from jax.experimental import pallas as pl
from jax.experimental.pallas import tpu as pltpu

F32 = jnp.float32
BF16 = jnp.bfloat16
HIGHEST = lax.Precision.HIGHEST

D_MODEL = 1024
GLA_HEADS = 4
GLA_DK = 64
GLA_DV = 128
GLA_KW = GLA_HEADS * GLA_DK
GLA_VW = GLA_HEADS * GLA_DV
DECAY_RANK = 16
GATE_NORMALIZER = 16.0
CHUNK = 64
CONV_CH = D_MODEL - GLA_VW
GRID_W = 64
N_EXPERTS = 8
EPS = 1e-6
LANES = 128
N_LEVELS = 6
MM_LEVELS = N_LEVELS - 1
MOD_ROWS = 24
VMEM_LIMIT = 52 * 1024 * 1024


def _cparams(sem):
    return pltpu.CompilerParams(dimension_semantics=sem, vmem_limit_bytes=VMEM_LIMIT)


def _rms(x, gain):
    return x * lax.rsqrt(jnp.mean(x * x, axis=-1, keepdims=True) + EPS) * gain


def _silu(x):
    return x / (1.0 + jnp.exp(-x))


def _dot_split(a, b):
    ah = a.astype(BF16)
    al = (a - ah.astype(F32)).astype(BF16)
    bh = b.astype(BF16)
    bl = (b - bh.astype(F32)).astype(BF16)
    dot = functools.partial(jnp.dot, preferred_element_type=F32)
    return dot(ah, bh) + (dot(ah, bl) + dot(al, bh))


def _mod_kernel(c_ref, w_ref, b_ref, o_ref):
    s = _silu(c_ref[...])
    o_ref[...] = jnp.dot(s, w_ref[...], precision=HIGHEST, preferred_element_type=F32) + b_ref[...]


def _modulation(c_all, w_mod, b_mod):
    depth, d, n = w_mod.shape
    tn = 1536
    return pl.pallas_call(
        _mod_kernel,
        grid=(depth, n // tn),
        in_specs=[
            pl.BlockSpec((MOD_ROWS, d), lambda i, j: (0, 0)),
            pl.BlockSpec((None, d, tn), lambda i, j: (i, 0, j)),
            pl.BlockSpec((None, 1, tn), lambda i, j: (i, 0, j)),
        ],
        out_specs=pl.BlockSpec((None, MOD_ROWS, tn), lambda i, j: (i, 0, j)),
        out_shape=jax.ShapeDtypeStruct((depth, MOD_ROWS, n), F32),
        compiler_params=_cparams(("parallel", "parallel")),
        name="modulation",
    )(c_all, w_mod, b_mod.reshape(depth, 1, n))


def _proj_kernel(x_ref, sh_ref, sc_ref, g_ref, wm_ref, wa_ref, wc_ref, wd_ref, bd_ref, cw_ref,
                 q_ref, k_ref, v_ref, gate_ref, gf_ref, gb_ref, yc_ref, *, seg, sub):
    for r0 in range(0, x_ref.shape[0], sub):
        rows = slice(r0, r0 + sub)
        h = _rms(x_ref[rows, :], g_ref[...]) * (1.0 + sc_ref[...]) + sh_ref[...]
        hb = h.astype(BF16)
        tile = lambda c0: jnp.dot(hb, wm_ref[:, c0:c0 + MXU_TILE], preferred_element_type=F32)
        q_ref[rows, :] = (tile(0) * (GLA_DK ** -0.5)).astype(BF16)
        k_ref[rows, :] = tile(GLA_KW).astype(BF16)
        for c0 in range(0, GLA_VW, MXU_TILE):
            v_ref[rows, c0:c0 + MXU_TILE] = tile(2 * GLA_KW + c0).astype(BF16)
            gate_ref[rows, c0:c0 + MXU_TILE] = tile(2 * GLA_KW + GLA_VW + c0).astype(BF16)
        pa = jnp.dot(hb, wa_ref[...], preferred_element_type=F32)
        xd = _dot_split(pa, wd_ref[...]) + bd_ref[...]
        ls = (jnp.minimum(xd, 0.0) - jnp.log1p(jnp.exp(-jnp.abs(xd)))) * (LOG2E / GATE_NORMALIZER)
        gf_ref[rows, :] = ls[:, :GLA_KW]
        gb_ref[rows, :] = ls[:, GLA_KW:]
        row = lax.broadcasted_iota(jnp.int32, (sub, MXU_TILE), 0) & (seg - 1)
        for c0 in range(0, CONV_CH, MXU_TILE):
            cols = slice(c0, c0 + MXU_TILE)
            part = lambda j: jnp.dot(hb, wc_ref[:, j * CONV_CH + c0:j * CONV_CH + c0 + MXU_TILE],
                                     preferred_element_type=F32)
            u = part(1) * part(2)
            u_prev = jnp.where(row == 0, 0.0, pltpu.roll(u, 1, 0))
            u_next = jnp.where(row == seg - 1, 0.0, pltpu.roll(u, sub - 1, 0))
            cw = cw_ref[:, cols]
            yc = part(0) * (cw[0:1] * u_prev + cw[1:2] * u + cw[2:3] * u_next)
            yc_ref[rows, cols] = yc.astype(BF16)


def _project(x2, mod3, mod_row, gain, wm, wa, wc, wd, bd, cw, *, tm, seg):
    n = x2.shape[0]
    d = D_MODEL
    row = lambda c: pl.BlockSpec((None, 1, d), lambda i: (mod_row(i), 0, c))
    const = lambda a: pl.BlockSpec(a.shape, lambda i: (0,) * a.ndim, pipeline_mode=pl.Buffered(1))
    tile = lambda w: pl.BlockSpec((tm, w), lambda i: (i, 0))
    shapes = [(GLA_KW, BF16), (GLA_KW, BF16), (GLA_VW, BF16), (GLA_VW, BF16),
              (GLA_KW, F32), (GLA_KW, F32), (CONV_CH, BF16)]
    return pl.pallas_call(
        functools.partial(_proj_kernel, seg=seg, sub=min(tm, 512)),
        grid=(n // tm,),
        in_specs=[tile(d), row(0), row(1), const(gain), const(wm), const(wa), const(wc),
                  const(wd), const(bd), const(cw)],
        out_specs=[tile(w) for w, _ in shapes],
        out_shape=[jax.ShapeDtypeStruct((n, w), t) for w, t in shapes],
        compiler_params=_cparams(("parallel",)),
        name="in_proj",
    )(x2, mod3, mod3, gain, wm, wa, wc, wd, bd, cw)


def _gla_constants():
    c = CHUNK
    main = np.zeros((2 * MM_LEVELS + 2, c, 2 * c), np.float32)
    lmask = np.zeros((N_LEVELS + 1, c, c), np.float32)
    for lvl in range(N_LEVELS):
        s = c >> (lvl + 1)
        for i in range(c):
            mid = (i // (2 * s)) * 2 * s + s
            if lvl < MM_LEVELS and i >= mid:
                main[2 * lvl, i, mid:i + 1] = 1.0
                main[2 * lvl + 1, i, c + mid:c + i] = 1.0
            elif lvl < MM_LEVELS:
                main[2 * lvl, i, c + i:c + mid] = 1.0
                main[2 * lvl + 1, i, i + 1:mid] = 1.0
            for j in range(c):
                same = (j // (2 * s)) == (i // (2 * s))
                lmask[lvl, i, j] = float(same and ((i >= mid) != (j >= mid)))
    lmask[N_LEVELS] = 2.0 * np.eye(c)
    bwd = np.zeros((2, c, c), np.float32)
    for i in range(c):
        main[2 * MM_LEVELS, i, :i + 1] = 1.0
        main[2 * MM_LEVELS + 1, i, i + 1:c] = 1.0
        bwd[0, i, i:] = 1.0
        bwd[1, i, :i] = 1.0
    main = main.reshape(-1, 2 * c)
    bwd = bwd.reshape(-1, c)
    lmask = np.concatenate([lmask, lmask], axis=-1)
    return jnp.asarray(main, BF16), jnp.asarray(bwd, BF16), jnp.asarray(lmask, F32)


def _sum_dot(n_ref, g, parts):
    gb = g.astype(BF16)
    rows = n_ref.shape[0] // parts
    outs = [jnp.dot(n_ref[p * rows:(p + 1) * rows, :], gb, preferred_element_type=F32) for p in range(parts)]
    per = rows // CHUNK
    return lambda b: outs[b // per][(b % per) * CHUNK:(b % per + 1) * CHUNK]


_NT = (((1,), (1,)), ((), ()))
_TN = (((0,), (0,)), ((), ()))


def _gla_kernel(qf_ref, kf_ref, vf_ref, gff_ref, gbf_ref, qb_ref, kb_ref, vb_ref, gbb_ref,
                gate_ref, gain_ref, s0f_ref, s0b_ref, nmain_ref, nbwd_ref, lmask_ref,
                out_ref, sf_out_ref, sb_out_ref, acc_ref, sf_ref, sb_ref, *, cps):
    n = pl.program_id(1)
    nb = pl.num_programs(1)
    c = CHUNK
    pw = 2 * GLA_DK
    vw = 2 * GLA_DV

    @pl.when(n == 0)
    def _():
        acc_ref[...] = jnp.zeros_like(acc_ref)
        sf_ref[...] = s0f_ref[...]
        sb_ref[...] = s0b_ref[...]

    klane = lax.broadcasted_iota(jnp.int32, (c, GLA_KW), 1)
    k_even = (klane & (pw - 1)) < GLA_DK
    vlane = lax.broadcasted_iota(jnp.int32, (c, vw), 1)
    v_low = vlane < GLA_DV
    srow = lax.broadcasted_iota(jnp.int32, (vw, pw), 0)
    scol = lax.broadcasted_iota(jnp.int32, (vw, pw), 1)
    s_diag = (srow < GLA_DV) == (scol < GLA_DK)
    zero_b = jnp.zeros((), BF16)

    def inter(qi, ki, dec, v, s_ref):
        outs = []
        for p in range(2):
            st = s_ref[p]
            outs.append(lax.dot_general(qi[:, p * pw:(p + 1) * pw], st.astype(BF16), _NT,
                                        preferred_element_type=F32))
            upd = lax.dot_general(v[:, p * vw:(p + 1) * vw], ki[:, p * pw:(p + 1) * pw], _TN,
                                  preferred_element_type=F32)
            s_ref[p] = st * dec[:, p * pw:(p + 1) * pw] + jnp.where(s_diag, upd, 0.0)
        return jnp.concatenate(outs, axis=1)

    odd_row = (lax.broadcasted_iota(jnp.int32, (c, GLA_KW), 0) & 1) == 1

    def fwd_chunk(j):
        r0 = pl.multiple_of(j * c, c)
        rows = pl.ds(r0, c)
        qb16 = qf_ref[rows, :]
        kb16 = kf_ref[rows, :]
        v = vf_ref[rows, :]
        q = qb16.astype(F32)
        k = kb16.astype(F32)
        gf = gff_ref[rows, :]
        gb = gbf_ref[rows, :]
        e = _sum_dot(nmain_ref, jnp.concatenate([gf, gb], axis=0), 2)
        att = [jnp.zeros((c, pw), F32), jnp.zeros((c, pw), F32)]
        for lvl in range(N_LEVELS + 1):
            if lvl < MM_LEVELS:
                qt = (q * jnp.exp2(e(2 * lvl))).astype(BF16)
                kt = (k * jnp.exp2(e(2 * lvl + 1))).astype(BF16)
            elif lvl < N_LEVELS:
                qt, kt = (q * jnp.exp2(jnp.where(odd_row, gf, gb))).astype(BF16), kb16
            else:
                qt, kt = qb16, kb16
            ke = jnp.where(k_even, kt, zero_b)
            ko = jnp.where(k_even, zero_b, kt)
            m = lmask_ref[lvl]
            for p in range(2):
                kbd = jnp.concatenate([ke[:, p * pw:(p + 1) * pw], ko[:, p * pw:(p + 1) * pw]], axis=0)
                att[p] = att[p] + m * lax.dot_general(qt[:, p * pw:(p + 1) * pw], kbd, _NT,
                                                      preferred_element_type=F32)
        outs = []
        for p in range(2):
            vp = v[:, p * vw:(p + 1) * vw]
            vbd = jnp.concatenate([jnp.where(v_low, vp, zero_b), jnp.where(v_low, zero_b, vp)], axis=0)
            outs.append(jnp.dot(att[p].astype(BF16), vbd, preferred_element_type=F32))
        o = jnp.concatenate(outs, axis=1)
        cum = e(2 * MM_LEVELS)
        qi = (q * jnp.exp2(cum)).astype(BF16)
        ki = (k * jnp.exp2(e(2 * MM_LEVELS + 1))).astype(BF16)
        o = o + inter(qi, ki, jnp.exp2(cum[c - 1:c]), v, sf_ref)
        a0 = pl.multiple_of(n * (cps * c) + r0, c)
        acc_ref[pl.ds(a0, c), :] += o

    def bwd_chunk(j):
        r0 = pl.multiple_of((cps - 1 - j) * c, c)
        rows = pl.ds(r0, c)
        q = qb_ref[rows, :].astype(F32)
        k = kb_ref[rows, :].astype(F32)
        v = vb_ref[rows, :]
        e = _sum_dot(nbwd_ref, gbb_ref[rows, :], 1)
        cum = e(0)
        qi = (q * jnp.exp2(cum)).astype(BF16)
        ki = (k * jnp.exp2(e(1))).astype(BF16)
        o = inter(qi, ki, jnp.exp2(cum[0:1]), v, sb_ref)
        a0 = pl.multiple_of((nb - 1 - n) * (cps * c) + r0, c)
        acc_ref[pl.ds(a0, c), :] += o

    def both(j, carry):
        fwd_chunk(j)
        bwd_chunk(j)
        return carry

    lax.fori_loop(0, cps, both, 0, unroll=min(cps, 4))

    @pl.when(n == nb - 1)
    def _():
        sf_out_ref[...] = sf_ref[...]
        sb_out_ref[...] = sb_ref[...]
        gain = gain_ref[...]
        rb = 256

        def gate_rows(i, carry):
            rows = pl.ds(pl.multiple_of(i * rb, rb), rb)
            o = acc_ref[rows, :]
            gt = gate_ref[rows, :].astype(F32)
            normed = jnp.concatenate(
                [_rms(o[:, h * GLA_DV:(h + 1) * GLA_DV], gain) for h in range(GLA_HEADS)], axis=1)
            out_ref[rows, :] = (normed * _silu(gt)).astype(BF16)
            return carry

        lax.fori_loop(0, acc_ref.shape[0] // rb, gate_rows, 0)


def _gla(q, k, v, gf, gb, gate, gain, s0f, s0b, consts, *, cps):
    b, l, _ = q.shape
    t = cps * CHUNK
    nb = l // t
    nmain, nbwd, lmask = consts
    fwd = lambda w: pl.BlockSpec((None, t, w), lambda i, n: (i, n, 0))
    bwd = lambda w: pl.BlockSpec((None, t, w), lambda i, n: (i, nb - 1 - n, 0))
    whole = lambda w: pl.BlockSpec((None, l, w), lambda i, n: (i, 0, 0))
    state = pl.BlockSpec((None, 2, 2 * GLA_DV, 2 * GLA_DK), lambda i, n: (i, 0, 0, 0))
    const = lambda a: pl.BlockSpec(a.shape, lambda i, n: (0,) * a.ndim)
    sshape = jax.ShapeDtypeStruct((b, 2, 2 * GLA_DV, 2 * GLA_DK), F32)
    return pl.pallas_call(
        functools.partial(_gla_kernel, cps=cps),
        grid=(b, nb),
        in_specs=[fwd(GLA_KW), fwd(GLA_KW), fwd(GLA_VW), fwd(GLA_KW), fwd(GLA_KW),
                  bwd(GLA_KW), bwd(GLA_KW), bwd(GLA_VW), bwd(GLA_KW),
                  whole(GLA_VW), const(gain), state, state, const(nmain), const(nbwd), const(lmask)],
        out_specs=[whole(GLA_VW), state, state],
        out_shape=[jax.ShapeDtypeStruct((b, l, GLA_VW), BF16), sshape, sshape],
        scratch_shapes=[pltpu.VMEM((l, GLA_VW), F32),
                        pltpu.VMEM((2, 2 * GLA_DV, 2 * GLA_DK), F32),
                        pltpu.VMEM((2, 2 * GLA_DV, 2 * GLA_DK), F32)],
        compiler_params=_cparams(("parallel", "arbitrary")),
        name="gla_scan",
    )(q, k, v, gf, gb, q, k, v, gb, gate, gain, s0f, s0b, nmain, nbwd, lmask)


MXU_TILE = 256


def _ff_chunks(ff, tf):
    return [(f0, min(f0 + tf, ff)) for f0 in range(0, ff, tf)]


def _swiglu(x, w1_ref, w3_ref, w2_ref, tf):
    ms = []
    for f0, f1 in _ff_chunks(w1_ref.shape[1], tf):
        a = jnp.dot(x, w1_ref[:, f0:f1], preferred_element_type=F32)
        b = jnp.dot(x, w3_ref[:, f0:f1], preferred_element_type=F32)
        ms.append((_silu(a) * b).astype(BF16))
    return jnp.dot(jnp.concatenate(ms, axis=1), w2_ref[...], preferred_element_type=F32)


F8 = jnp.float8_e4m3fn
F8_PEAK = 256.0
TINY = 1e-30
LOG2E = 1.4426950408889634


def _quant_kernel(w_ref, q_ref, inv_ref, *, rb):
    nblk = w_ref.shape[0] // rb

    def peak(i, m):
        blk = jnp.abs(w_ref[pl.ds(pl.multiple_of(i * rb, rb), rb), :])
        return jnp.maximum(m, jnp.max(blk, axis=0, keepdims=True))

    colmax = lax.fori_loop(0, nblk, peak, jnp.zeros((1, w_ref.shape[1]), F32))
    amax = jnp.maximum(jnp.max(colmax, axis=1, keepdims=True), TINY)
    scale = F8_PEAK / amax

    def cast(i, carry):
        rows = pl.ds(pl.multiple_of(i * rb, rb), rb)
        q_ref[rows, :] = (w_ref[rows, :] * scale).astype(F8)
        return carry

    lax.fori_loop(0, nblk, cast, 0)
    inv_ref[...] = jnp.broadcast_to(amax * (1.0 / F8_PEAK), inv_ref.shape)


def _quantize(w):
    ne, r, c = w.shape
    return pl.pallas_call(
        functools.partial(_quant_kernel, rb=128),
        grid=(ne,),
        in_specs=[pl.BlockSpec((None, r, c), lambda e: (e, 0, 0))],
        out_specs=[pl.BlockSpec((None, r, c), lambda e: (e, 0, 0)), pl.BlockSpec((None, 1, LANES), lambda e: (e, 0, 0))],
        out_shape=[jax.ShapeDtypeStruct((ne, r, c), F8), jax.ShapeDtypeStruct((ne, 1, LANES), F32)],
        compiler_params=_cparams(("parallel",)),
        name="quantize_weights",
    )(w)


def _row_scale(parts):
    amax = functools.reduce(jnp.maximum, [jnp.max(jnp.abs(p), axis=1, keepdims=True) for p in parts])
    amax = jnp.maximum(amax, TINY)
    return F8_PEAK / amax, amax * (1.0 / F8_PEAK)


def _swiglu_f8(xs, w1_ref, w3_ref, w2_ref, inv1, inv3, inv2, tf):
    k = xs[0].shape[1]
    sx, ix = _row_scale(xs)
    x8 = [(x * sx).astype(F8) for x in xs]
    c1 = ix * inv1
    ms = []
    for f0, f1 in _ff_chunks(w1_ref.shape[1], tf):
        a = sum(jnp.dot(x, w1_ref[r * k:(r + 1) * k, f0:f1], preferred_element_type=F32) for r, x in enumerate(x8))
        b = sum(jnp.dot(x, w3_ref[r * k:(r + 1) * k, f0:f1], preferred_element_type=F32) for r, x in enumerate(x8))
        ms.append(a * b / (1.0 + jnp.exp2(a * (c1 * -LOG2E))))
    sm, im = _row_scale(ms)
    m8 = jnp.concatenate([(m * sm).astype(F8) for m in ms], axis=1)
    return jnp.dot(m8, w2_ref[...], preferred_element_type=F32) * (im * c1 * (ix * inv3) * inv2)


def _token_mix_residual(x_ref, gla_ref, yc_ref, gt_ref, gp_ref, wo_ref, rows=slice(None)):
    y = (jnp.dot(gla_ref[rows, :], wo_ref[:GLA_VW, :], preferred_element_type=F32)
         + jnp.dot(yc_ref[rows, :], wo_ref[GLA_VW:, :], preferred_element_type=F32))
    return x_ref[rows, :] + gt_ref[...] * _rms(y, gp_ref[...])


def _mix_specs(mod_row, gpm, wo, tm):
    d = D_MODEL
    return [pl.BlockSpec((tm, d), lambda i: (i, 0)), pl.BlockSpec((tm, GLA_VW), lambda i: (i, 0)),
            pl.BlockSpec((tm, CONV_CH), lambda i: (i, 0)),
            pl.BlockSpec((None, 1, d), lambda i: (mod_row(i), 0, 2)),
            pl.BlockSpec(gpm.shape, lambda i: (0, 0)),
            pl.BlockSpec(wo.shape, lambda i: (0, 0), pipeline_mode=pl.Buffered(1))]


def _dense_kernel(x_ref, gla_ref, yc_ref, gt1_ref, gpm_ref, wo_ref, sh_ref, sc_ref, gt_ref, gpre_ref, gpost_ref,
                  w1_ref, w3_ref, w2_ref, o_ref, *, tf, sub):
    for r0 in range(0, x_ref.shape[0], sub):
        rows = slice(r0, r0 + sub)
        x1 = _token_mix_residual(x_ref, gla_ref, yc_ref, gt1_ref, gpm_ref, wo_ref, rows)
        h = _rms(x1, gpre_ref[...]) * (1.0 + sc_ref[...]) + sh_ref[...]
        y = _swiglu(h.astype(BF16), w1_ref, w3_ref, w2_ref, tf)
        o_ref[rows, :] = x1 + gt_ref[...] * _rms(y, gpost_ref[...])


def _dense_mix(x2, gla, yc, mod3, mod_row, gpm, wo, gpre, gpost, w1, w3, w2, *, tm, tf):
    n, d = x2.shape
    row = lambda c: pl.BlockSpec((None, 1, d), lambda i: (mod_row(i), 0, c))
    const = lambda a: pl.BlockSpec(a.shape, lambda i: (0,) * a.ndim)
    held = lambda a: pl.BlockSpec(a.shape, lambda i: (0,) * a.ndim, pipeline_mode=pl.Buffered(1))
    return pl.pallas_call(
        functools.partial(_dense_kernel, tf=tf, sub=min(tm, 512)),
        grid=(n // tm,),
        in_specs=_mix_specs(mod_row, gpm, wo, tm) + [row(3), row(4), row(5), const(gpre), const(gpost),
                                                     held(w1), held(w3), held(w2)],
        out_specs=pl.BlockSpec((tm, d), lambda i: (i, 0)),
        out_shape=jax.ShapeDtypeStruct((n, d), F32),
        compiler_params=_cparams(("parallel",)),
        name="dense_mix",
    )(x2, gla, yc, mod3, gpm, wo, mod3, mod3, mod3, gpre, gpost, w1, w3, w2)


HALF = D_MODEL // 2
HI_MASK = 0xFFFF0000


def _pack_rows(v):
    bits = pltpu.bitcast(v.astype(BF16).astype(F32), jnp.uint32)
    return (bits[:, :HALF] >> 16) | (bits[:, HALF:] & jnp.uint32(HI_MASK))


def _unpack_rows(w):
    lo = pltpu.bitcast(w << 16, F32)
    hi = pltpu.bitcast(w & jnp.uint32(HI_MASK), F32)
    return jnp.concatenate([lo, hi], axis=1)


def _route_kernel(x_ref, gla_ref, yc_ref, gt1_ref, gpm_ref, wo_ref, sh_ref, sc_ref, gpre_ref, wr_ref, tri_ref,
                  x1_ref, hp_ref, info_ref, cnt_ref, carry_ref):
    i = pl.program_id(0)

    @pl.when(i == 0)
    def _():
        carry_ref[...] = jnp.zeros_like(carry_ref)

    sub = tri_ref.shape[0]
    carry = carry_ref[...]
    for r0 in range(0, x_ref.shape[0], sub):
        rows = slice(r0, r0 + sub)
        x1 = _token_mix_residual(x_ref, gla_ref, yc_ref, gt1_ref, gpm_ref, wo_ref, rows)
        x1_ref[rows, :] = x1
        h = _rms(x1, gpre_ref[...]) * (1.0 + sc_ref[...]) + sh_ref[...]
        hp_ref[rows, :] = _pack_rows(h)
        logits = _dot_split(h, wr_ref[...])
        lane = lax.broadcasted_iota(jnp.int32, logits.shape, 1).astype(F32)
        logits = jnp.where(lane < N_EXPERTS, logits, -jnp.inf)
        m1 = jnp.max(logits, axis=-1, keepdims=True)
        i1 = jnp.min(jnp.where(logits == m1, lane, float(LANES)), axis=-1, keepdims=True)
        rest = jnp.where(lane == i1, -jnp.inf, logits)
        m2 = jnp.max(rest, axis=-1, keepdims=True)
        i2 = jnp.min(jnp.where(rest == m2, lane, float(LANES)), axis=-1, keepdims=True)
        e2 = jnp.exp(m2 - m1)
        den = 1.0 + e2
        pick = jnp.where((lane == i1) | (lane == i2), 1.0, 0.0)
        rank = jnp.dot(tri_ref[...], pick.astype(BF16), preferred_element_type=F32) + carry
        carry = carry + jnp.sum(pick, axis=0, keepdims=True)
        r1 = jnp.sum(jnp.where(lane == i1, rank, 0.0), axis=-1, keepdims=True)
        r2 = jnp.sum(jnp.where(lane == i2, rank, 0.0), axis=-1, keepdims=True)
        info = jnp.zeros_like(logits)
        for col, val in enumerate((i1, i2, 1.0 / den, e2 / den, r1, r2)):
            info = jnp.where(lane == col, val, info)
        info_ref[rows, :] = info
    carry_ref[...] = carry
    cnt_ref[...] = carry


def _route(x2, gla, yc, mod3, mod_row, gpm, wo, gpre, w_r, *, tm):
    n, d = x2.shape
    row = lambda c: pl.BlockSpec((None, 1, d), lambda i: (mod_row(i), 0, c))
    const = lambda a: pl.BlockSpec(a.shape, lambda i: (0,) * a.ndim)
    sub = min(tm, 256)
    tri = jnp.asarray(np.tril(np.ones((sub, sub), np.float32), -1), BF16)
    return pl.pallas_call(
        _route_kernel,
        grid=(n // tm,),
        in_specs=_mix_specs(mod_row, gpm, wo, tm) + [row(3), row(4), const(gpre), const(w_r), const(tri)],
        out_specs=[pl.BlockSpec((tm, d), lambda i: (i, 0)), pl.BlockSpec((tm, HALF), lambda i: (i, 0)),
                   pl.BlockSpec((tm, LANES), lambda i: (i, 0)), pl.BlockSpec((1, LANES), lambda i: (0, 0))],
        out_shape=[jax.ShapeDtypeStruct((n, d), F32), jax.ShapeDtypeStruct((n, HALF), jnp.uint32),
                   jax.ShapeDtypeStruct((n, LANES), F32), jax.ShapeDtypeStruct((1, LANES), F32)],
        scratch_shapes=[pltpu.VMEM((1, LANES), F32)],
        compiler_params=_cparams(("arbitrary",)),
        name="moe_route",
    )(x2, gla, yc, mod3, gpm, wo, mod3, mod3, gpre, w_r, tri)


def _row_copy(src, s, dst, t, sem):
    return pltpu.make_async_copy(src.at[pl.ds(s, 1), :], dst.at[pl.ds(t, 1), :], sem)


DMA_UNROLL = 8


def _dispatch_kernel(pos_ref, hp_ref, xs_in_ref, xs_ref, sem, *, tm):
    del xs_in_ref

    def start(t, carry):
        _row_copy(hp_ref, t, xs_ref, pos_ref[0, 2 * t], sem).start(priority=0)
        _row_copy(hp_ref, t, xs_ref, pos_ref[0, 2 * t + 1], sem).start(priority=1)
        return carry

    def wait(t, carry):
        _row_copy(hp_ref, 0, xs_ref, 0, sem).wait()
        return carry

    lax.fori_loop(0, tm, start, 0, unroll=DMA_UNROLL)
    lax.fori_loop(0, 2 * tm, wait, 0, unroll=DMA_UNROLL)


def _dispatch(pos3, hp, n_sorted, *, tm):
    n = hp.shape[0]
    xs0 = jnp.zeros((n_sorted, HALF), jnp.uint32)
    return pl.pallas_call(
        functools.partial(_dispatch_kernel, tm=tm),
        grid=(n // tm,),
        in_specs=[pl.BlockSpec((None, 1, 2 * tm), lambda i: (i, 0, 0), memory_space=pltpu.SMEM),
                  pl.BlockSpec((tm, HALF), lambda i: (i, 0)), pl.BlockSpec(memory_space=pl.ANY)],
        out_specs=pl.BlockSpec(memory_space=pl.ANY),
        out_shape=jax.ShapeDtypeStruct((n_sorted, HALF), jnp.uint32),
        scratch_shapes=[pltpu.SemaphoreType.DMA],
        input_output_aliases={2: 0},
        compiler_params=pltpu.CompilerParams(dimension_semantics=("arbitrary",), has_side_effects=True),
        name="moe_dispatch",
    )(pos3, hp, xs0)


def _expert_kernel(te_ref, nu_ref, xs_ref, w1_ref, w3_ref, w2_ref, i1_ref, i3_ref, i2_ref, ys_ref, *, tf):
    del te_ref

    @pl.when(pl.program_id(0) < nu_ref[0])
    def _():
        xw = xs_ref[...]
        lo = pltpu.bitcast(xw << 16, F32)
        hi = pltpu.bitcast(xw & jnp.uint32(HI_MASK), F32)
        inv = [r[:, :1] for r in (i1_ref[...], i3_ref[...], i2_ref[...])]
        ys_ref[...] = _pack_rows(_swiglu_f8([lo, hi], w1_ref, w3_ref, w2_ref, *inv, tf))

    @pl.when(pl.program_id(0) >= nu_ref[0])
    def _():
        ys_ref[...] = jnp.zeros_like(ys_ref)


def _experts(tile_expert, n_used, xs, w1, w3, w2, *, tmg, tf):
    n_sorted = xs.shape[0]
    _, d, ff = w1.shape
    (q1, i1), (q3, i3), (q2, i2) = _quantize(w1), _quantize(w3), _quantize(w2)
    blk = lambda j, te, nu: (jnp.minimum(j, nu[0] - 1), 0)
    held = lambda r, c: pl.BlockSpec((None, r, c), lambda j, te, nu: (te[j], 0, 0), pipeline_mode=pl.Buffered(1))
    grid_spec = pltpu.PrefetchScalarGridSpec(
        num_scalar_prefetch=2,
        grid=(n_sorted // tmg,),
        in_specs=[pl.BlockSpec((tmg, HALF), blk), held(d, ff), held(d, ff), held(ff, d),
                  held(1, LANES), held(1, LANES), held(1, LANES)],
        out_specs=pl.BlockSpec((tmg, HALF), lambda j, te, nu: (j, 0)),
    )
    return pl.pallas_call(
        functools.partial(_expert_kernel, tf=tf),
        grid_spec=grid_spec,
        out_shape=jax.ShapeDtypeStruct((n_sorted, HALF), jnp.uint32),
        compiler_params=_cparams(("arbitrary",)),
        name="moe_experts",
    )(tile_expert, n_used, xs, q1, q3, q2, i1, i3, i2)


def _combine_kernel(pos_ref, x_ref, gt_ref, gpost_ref, info_ref, ys_ref, o_ref, buf_ref, sems, *, tm):
    i = pl.program_id(0)
    slot = i & 1

    @pl.when(i < pl.num_programs(0) - 1)
    def _():
        def start(t, carry):
            _row_copy(ys_ref, pos_ref[0, 2 * t], buf_ref.at[slot, 0], t, sems.at[slot]).start(priority=0)
            _row_copy(ys_ref, pos_ref[0, 2 * t + 1], buf_ref.at[slot, 1], t, sems.at[slot]).start(priority=1)
            return carry

        lax.fori_loop(0, tm, start, 0, unroll=DMA_UNROLL)

    @pl.when(i > 0)
    def _():
        prev = 1 - slot

        def wait(t, carry):
            _row_copy(ys_ref, 0, buf_ref.at[prev, 0], 0, sems.at[prev]).wait()
            return carry

        lax.fori_loop(0, 2 * tm, wait, 0, unroll=DMA_UNROLL)
        info = info_ref[...]
        lane = lax.broadcasted_iota(jnp.int32, info.shape, 1)
        wt1 = jnp.sum(jnp.where(lane == 2, info, 0.0), axis=-1, keepdims=True)
        wt2 = jnp.sum(jnp.where(lane == 3, info, 0.0), axis=-1, keepdims=True)
        y = wt1 * _unpack_rows(buf_ref[prev, 0]) + wt2 * _unpack_rows(buf_ref[prev, 1])
        o_ref[...] = x_ref[...] + gt_ref[...] * _rms(y, gpost_ref[...])


def _combine(pos3, x2, mod3, mod_row, gpost, info, ys, *, tm):
    n, d = x2.shape
    nt = n // tm
    done = lambda i: jnp.maximum(i - 1, 0)
    tile = pl.BlockSpec((tm, d), lambda i: (done(i), 0))
    return pl.pallas_call(
        functools.partial(_combine_kernel, tm=tm),
        grid=(nt + 1,),
        in_specs=[pl.BlockSpec((None, 1, 2 * tm), lambda i: (jnp.minimum(i, nt - 1), 0, 0),
                               memory_space=pltpu.SMEM),
                  tile, pl.BlockSpec((None, 1, d), lambda i: (mod_row(done(i)), 0, 5)),
                  pl.BlockSpec(gpost.shape, lambda i: (0, 0)),
                  pl.BlockSpec((tm, LANES), lambda i: (done(i), 0)), pl.BlockSpec(memory_space=pl.ANY)],
        out_specs=tile,
        out_shape=jax.ShapeDtypeStruct((n, d), F32),
        scratch_shapes=[pltpu.VMEM((2, 2, tm, HALF), jnp.uint32), pltpu.SemaphoreType.DMA((2,))],
        compiler_params=_cparams(("arbitrary",)),
        name="moe_combine",
    )(pos3, x2, mod3, gpost, info, ys)


def _moe_mix(x2, gla, yc, mod3, mod_row, gpm, wo, gpre, gpost, w_r, w1, w3, w2, *, tm, tmg, tf):
    n = x2.shape[0]
    x2, hp, info, cnt = _route(x2, gla, yc, mod3, mod_row, gpm, wo, gpre, w_r, tm=tm)
    counts = cnt[0, :N_EXPERTS].astype(jnp.int32)
    padded = ((counts + tmg - 1) // tmg) * tmg
    ends = jnp.cumsum(padded)
    starts = ends - padded
    n_tiles = (2 * n) // tmg + N_EXPERTS
    picks = info[:, 0:2].astype(jnp.int32)
    pos = starts[picks] + info[:, 4:6].astype(jnp.int32)
    pos3 = pos.reshape(n // tm, 1, 2 * tm)
    tile_expert = jnp.sum(jnp.arange(n_tiles, dtype=jnp.int32)[:, None] * tmg >= ends[None, :], axis=1)
    n_used = (ends[-1] // tmg).astype(jnp.int32).reshape(1)
    tile_expert = jnp.minimum(tile_expert, tile_expert[jnp.maximum(n_used[0] - 1, 0)]).astype(jnp.int32)
    xs = _dispatch(pos3, hp, n_tiles * tmg, tm=tm)
    ys = _experts(tile_expert, n_used, xs, w1, w3, w2, tmg=tmg, tf=tf)
    return _combine(pos3, x2, mod3, mod_row, gpost, info, ys, tm=tm)


def kernel(x, c, ctx, c_ctx, w_mod, b_mod, g_mix_pre, g_mix_post, w_in, w_decay, b_decay, gla_norm,
           conv_w, w_out, g_ffn_pre, g_ffn_post, w1, w3, w2, w_router, e_w1, e_w3, e_w2):
    bsz, seq, d = x.shape
    ctx_len = ctx.shape[1]
    depth = w_mod.shape[0]
    assert d == D_MODEL and ctx_len % CHUNK == 0 and bsz + 1 <= MOD_ROWS

    c_all = jnp.zeros((MOD_ROWS, d), F32).at[:bsz].set(c).at[bsz].set(c_ctx)
    mod = _modulation(c_all, w_mod, b_mod)
    consts = _gla_constants()

    tm_x = 512
    tm_p = 1024
    tm_c = ctx_len
    assert seq % tm_p == 0 and seq % tm_x == 0 and ctx_len & (ctx_len - 1) == 0
    x_row = lambda i: i // (seq // tm_x)
    c_row = lambda i: bsz
    x2 = x.reshape(bsz * seq, d)
    xc2 = ctx.reshape(bsz * ctx_len, d)
    zero_state = jnp.zeros((bsz, 2, 2 * GLA_DV, 2 * GLA_DK), F32)
    row2 = lambda a: a.reshape(1, -1)

    o_q, o_k, o_v = 0, GLA_KW, 2 * GLA_KW
    o_g = o_v + GLA_VW
    o_a = o_g + GLA_VW
    o_c = o_a + 2 * DECAY_RANK

    for i in range(depth):
        last = i == depth - 1
        mod3 = mod[i].reshape(MOD_ROWS, 1, 6 * d)
        wi = w_in[i]
        wm = wi[:, :o_a].astype(BF16)
        wa = jnp.zeros((d, LANES), F32).at[:, :2 * DECAY_RANK].set(wi[:, o_a:o_c]).astype(BF16)
        wc = wi[:, o_c:].astype(BF16)
        wd = jnp.zeros((LANES, 2 * GLA_KW), F32)
        wd = wd.at[:DECAY_RANK, :GLA_KW].set(w_decay[i, 0]).at[DECAY_RANK:2 * DECAY_RANK, GLA_KW:].set(w_decay[i, 1])
        bd = b_decay[i].reshape(1, 2 * GLA_KW)
        wo = w_out[i].astype(BF16)
        gain = row2(gla_norm[i])

        def mix(tokens, mod_row, tm, seg, nseq, slen, s0f, s0b, cps):
            q, k, v, gate, gf, gb, yc = _project(tokens, mod3, mod_row, row2(g_mix_pre[i]), wm, wa, wc, wd, bd,
                                                 conv_w[i], tm=tm, seg=seg)
            r3 = lambda a: a.reshape(nseq, slen, a.shape[-1])
            o, sf, sb = _gla(r3(q), r3(k), r3(v), r3(gf), r3(gb), r3(gate), gain, s0f, s0b, consts, cps=cps)
            return o.reshape(nseq * slen, GLA_VW), yc, sf, sb

        o_c_, yc_c, s_f, s_b = mix(xc2, c_row, tm_c, ctx_len, bsz, ctx_len, zero_state, zero_state,
                                   ctx_len // CHUNK)
        o_x, yc_x, _, _ = mix(x2, lambda t: t // (seq // tm_p), tm_p, GRID_W, bsz, seq, s_f, s_b, 8)

        j = i // 2
        common = dict(gpm=row2(g_mix_post[i]), wo=wo, gpre=row2(g_ffn_pre[i]), gpost=row2(g_ffn_post[i]))
        if i % 2 == 0:
            ffn = functools.partial(_dense_mix, w1=w1[j].astype(BF16), w3=w3[j].astype(BF16),
                                    w2=w2[j].astype(BF16), tf=MXU_TILE, **common)
        else:
            w_r = jnp.zeros((d, LANES), F32).at[:, :N_EXPERTS].set(w_router[j])
            ffn = functools.partial(_moe_mix, w_r=w_r, w1=e_w1[j], w3=e_w3[j], w2=e_w2[j], tmg=512,
                                    tf=MXU_TILE, **common)
        x2 = ffn(x2, o_x, yc_x, mod3, x_row, tm=tm_x)
        if not last:
            xc2 = ffn(xc2, o_c_, yc_c, mod3, c_row, tm=tm_c)
    return x2.reshape(bsz, seq, d)
```
